```python
import math
import jax
import jax.numpy as jnp
from jax import lax
import numpy as np

D_MODEL = 2048
BATCH = 16
SEQ = 256
DEPTH = 2
DEC_BATCH = 8
DEC_SEQ = 2048
PAST_LEN = 256

GRID_W = 64
W_RWKV = 512
W_HYENA = 512
W_ATTN = 1024
RWKV_HEAD = 64
N_RWKV_HEADS = W_RWKV // RWKV_HEAD
HEAD_DIM = 64
N_Q_HEADS = W_ATTN // HEAD_DIM
N_KV_HEADS = 4
KV_GROUP = N_Q_HEADS // N_KV_HEADS
KV_WIDTH = N_KV_HEADS * HEAD_DIM
LORA_W = 64
LORA_A = 64
LORA_G = 128
RWKV_SIZES = (W_RWKV, W_RWKV, W_RWKV, LORA_W, LORA_W, LORA_A, LORA_A, LORA_G)
RWKV_SPLITS = tuple(sum(RWKV_SIZES[:i + 1]) for i in range(len(RWKV_SIZES) - 1))
RWKV_COLS = sum(RWKV_SIZES)
HYENA_COLS = 3 * W_HYENA
ATTN_COLS = W_ATTN + 2 * KV_WIDTH
IN_COLS = RWKV_COLS + HYENA_COLS + ATTN_COLS
HYENA_BANDS = 16
HYENA_EMB = 1 + 2 * HYENA_BANDS
HYENA_FFN = 64
D_FF = 5632
Q_BLOCK = 128
ROPE_THETA = 10000.0
ALPHA = (2 * DEPTH) ** 0.25
BETA = (8 * DEPTH) ** -0.25
LN_EPS = 1e-5
QK_EPS = 1e-6
GN_EPS = 64e-5

kernel_name = 'hybrid_rwkv7_hyena_gqa_diffusion_step'


def layer_norm(x, g, b):
    xf = x.astype(jnp.float32)
    mu = jnp.mean(xf, -1, keepdims=True)
    var = jnp.mean(jnp.square(xf - mu), -1, keepdims=True)
    return ((xf - mu) * lax.rsqrt(var + LN_EPS) * g + b).astype(x.dtype)


def rms_norm(x, g):
    xf = x.astype(jnp.float32)
    return (xf * lax.rsqrt(jnp.mean(jnp.square(xf), -1, keepdims=True) + QK_EPS) * g).astype(x.dtype)


def dwconv3(x, w):
    xp = jnp.pad(x, ((0, 0), (1, 1), (0, 0)))
    return xp[:, :-2] * w[0] + xp[:, 1:-1] * w[1] + xp[:, 2:] * w[2]


def axial_rope(n_tokens):
    f32 = jnp.float32
    rows = n_tokens // GRID_W
    row = jnp.repeat(jnp.arange(rows, dtype=f32), GRID_W)
    col = jnp.tile(jnp.arange(GRID_W, dtype=f32), rows)
    n_freq = HEAD_DIM // 4
    inv = ROPE_THETA ** (-jnp.arange(n_freq, dtype=f32) / n_freq)
    ang = jnp.concatenate([row[:, None] * inv, col[:, None] * inv], -1)
    return jnp.cos(ang), jnp.sin(ang)


def apply_rope(x, cos, sin):
    xf = x.astype(jnp.float32)
    half = HEAD_DIM // 2
    x1, x2 = xf[..., :half], xf[..., half:]
    c, s = cos[None, :, None, :], sin[None, :, None, :]
    return jnp.concatenate([x1 * c - x2 * s, x2 * c + x1 * s], -1).astype(x.dtype)


def wkv_scan(s0, r, w, k, v, kk, a, reverse):
    def step(S, inp):
        r_t, w_t, k_t, v_t, kk_t, a_t = inp
        sa = jnp.einsum('bhvk,bhk->bhv', S, -kk_t)
        S = (S * w_t[:, :, None, :] + sa[..., None] * (kk_t * a_t)[:, :, None, :]
             + v_t[..., None] * k_t[:, :, None, :])
        return S, jnp.einsum('bhvk,bhk->bhv', S, r_t)
    xs = tuple(jnp.swapaxes(t, 0, 1) for t in (r, w, k, v, kk, a))
    s_fin, ys = lax.scan(step, s0, xs, reverse=reverse)
    return s_fin, jnp.swapaxes(ys, 0, 1)


def rwkv_mix(p, lp, s0_f, s0_b):
    f32 = jnp.float32
    p = dwconv3(p, lp['rwkv_shift'])
    r, k, v, wd_f, wd_b, ad_f, ad_b, gd = jnp.split(p, RWKV_SPLITS, axis=-1)
    B, L, _ = r.shape

    def heads(t):
        return t.astype(f32).reshape(B, L, N_RWKV_HEADS, RWKV_HEAD)

    g = jax.nn.sigmoid(gd) @ lp['rwkv_g2']
    kk = heads(k * lp['rwkv_kk'])
    kk = kk * lax.rsqrt(jnp.sum(jnp.square(kk), -1, keepdims=True) + 1e-12)
    rh, vh = heads(r), heads(v)
    r_k = lp['rwkv_rk'].astype(f32).reshape(N_RWKV_HEADS, RWKV_HEAD)
    y = 0.0
    bonus = 0.0
    finals = []
    for d, (wd, ad, s0, rev) in enumerate(((wd_f, ad_f, s0_f, False), (wd_b, ad_b, s0_b, True))):
        w_log = -jax.nn.softplus(-(lp['rwkv_w0'][d] + jnp.tanh(wd) @ lp['rwkv_w2'][d])) - 0.5
        decay = jnp.exp(-jnp.exp(heads(w_log)))
        a = jax.nn.sigmoid(lp['rwkv_a0'][d] + ad @ lp['rwkv_a2'][d])
        kd = heads(k * (1.0 + (a - 1.0) * lp['rwkv_ka']))
        s_fin, y_d = wkv_scan(s0.astype(f32), rh, decay, kd, vh, kk, heads(a), rev)
        y = y + y_d
        bonus = bonus + jnp.sum(rh * kd * r_k, -1, keepdims=True) * vh
        finals.append(s_fin)
    mu = jnp.mean(y, -1, keepdims=True)
    var = jnp.mean(jnp.square(y - mu), -1, keepdims=True)
    y = ((y - mu) * lax.rsqrt(var + GN_EPS)).reshape(B, L, W_RWKV) * lp['rwkv_gn_g'] + lp['rwkv_gn_b']
    out = (y + bonus.reshape(B, L, W_RWKV)) * g
    return out.astype(p.dtype), finals[0], finals[1]


def hyena_filters(n, lp):
    f32 = jnp.float32
    t01 = jnp.linspace(0.0, 1.0, n, dtype=f32)[:, None]
    pos = jnp.arange(n, dtype=f32)[:, None]
    bands = jnp.linspace(1e-4, HYENA_BANDS - 1, HYENA_BANDS, dtype=f32)[None, :]
    ang = (2.0 * math.pi / n) * pos * bands
    z = jnp.concatenate([t01, jnp.cos(ang), -jnp.sin(ang)], -1)
    freq = lp['hy_freq'].astype(f32)
    h = jnp.sin(freq[0] * (z @ lp['hy_w1'].astype(f32) + lp['hy_b1']))
    h = jnp.sin(freq[1] * (h @ lp['hy_w2'].astype(f32) + lp['hy_b2']))
    h = (h @ lp['hy_w3'].astype(f32)).reshape(n, 2, W_HYENA)
    h = h * jnp.exp(-t01[:, :, None] * jnp.abs(lp['hy_decay'].astype(f32))[None])
    return h[:, 0], h[:, 1]


def hyena_mix(p, lp):
    f32 = jnp.float32
    p = dwconv3(p, lp['hy_short'])
    x0, x1, v = jnp.split(p, 3, axis=-1)
    n = p.shape[1]
    h_fwd, h_bwd = hyena_filters(n, lp)
    filt_full = jnp.concatenate([h_fwd[:1] + h_bwd[:1], h_fwd[1:],
                                 jnp.zeros((1, W_HYENA), f32), h_bwd[:0:-1]], 0)
    u = (x1 * v).astype(f32)
    y = jnp.fft.irfft(jnp.fft.rfft(u, n=2 * n, axis=1) * jnp.fft.rfft(filt_full, axis=0)[None],
                      n=2 * n, axis=1)[:, :n]
    y = y + u * lp['hy_bias'].astype(f32)
    return (x0.astype(f32) * y).astype(p.dtype)


def block_attention(q, k, v):
    B, Lq = q.shape[0], q.shape[1]
    nb = Lq // Q_BLOCK
    qb = jnp.moveaxis(q.reshape(B, nb, Q_BLOCK, N_KV_HEADS, KV_GROUP, HEAD_DIM), 1, 0)
    scale = HEAD_DIM ** -0.5

    def one(qblk):
        s = jnp.einsum('bqhgd,bkhd->bhgqk', qblk, k, preferred_element_type=jnp.float32) * scale
        pr = jax.nn.softmax(s, axis=-1)
        return jnp.einsum('bhgqk,bkhd->bqhgd', pr.astype(v.dtype), v)

    out = lax.map(one, qb)
    return jnp.moveaxis(out, 0, 1).reshape(B, Lq, N_Q_HEADS * HEAD_DIM)


def attention_mix(p, lp, rope, ctx_k, ctx_v):
    q, k, v = jnp.split(p, [W_ATTN, W_ATTN + KV_WIDTH], axis=-1)
    B, L, _ = q.shape
    q = rms_norm(q.reshape(B, L, N_Q_HEADS, HEAD_DIM), lp['attn_qn'])
    k = rms_norm(k.reshape(B, L, N_KV_HEADS, HEAD_DIM), lp['attn_kn'])
    v = v.reshape(B, L, N_KV_HEADS, HEAD_DIM)
    if rope is None:
        return block_attention(q, k, v), k, v
    cos, sin = rope
    q = apply_rope(q, cos, sin)
    k = apply_rope(k, cos, sin)
    keys = jnp.concatenate([k, ctx_k.astype(k.dtype)], 1)
    vals = jnp.concatenate([v, ctx_v.astype(v.dtype)], 1)
    return block_attention(q, keys, vals), None, None


def trunk_layer(x, cond, lp, rope, ctx_k, ctx_v, s0_f, s0_b):
    mod = jax.nn.silu(cond) @ lp['w_mod'] + lp['b_mod']
    sh1, sc1, g1, sh2, sc2, g2 = jnp.split(mod[:, None, :], 6, axis=-1)
    h = x * (1.0 + sc1) + sh1
    proj = h @ lp['w_in']
    p_r, p_h, p_a = jnp.split(proj, [RWKV_COLS, RWKV_COLS + HYENA_COLS], axis=-1)
    y_r, s_f, s_b = rwkv_mix(p_r, lp, s0_f, s0_b)
    y_h = hyena_mix(p_h, lp)
    y_a, k_c, v_c = attention_mix(p_a, lp, rope, ctx_k, ctx_v)
    mix = jnp.concatenate([y_r, y_h.astype(y_r.dtype), y_a.astype(y_r.dtype)], -1) @ lp['w_out']
    x = layer_norm(ALPHA * x + g1 * mix, lp['ln1_g'], lp['ln1_b'])
    h2 = x * (1.0 + sc2) + sh2
    u = dwconv3(h2 @ lp['ffn_up'], lp['ffn_conv'])
    a, b = jnp.split(u, 2, axis=-1)
    f = (jax.nn.silu(a) * b) @ lp['ffn_down']
    x = layer_norm(ALPHA * x + g2 * f, lp['ln2_g'], lp['ln2_b'])
    return x, k_c, v_c, s_f, s_b


def setup_inputs(seed: int = 0) -> dict:
    key = jax.random.key(seed)
    keys = iter(jax.random.split(key, 48))
    f32 = jnp.float32

    def nrm(shape, scale=1.0):
        return scale * jax.random.normal(next(keys), shape, f32)

    def taps3(side, centre, n, noise):
        base = jnp.array([side, centre, side], f32)[None, :, None]
        return base + nrm((DEPTH, 3, n), noise)

    w0_base = jnp.linspace(-6.0, 1.0, W_RWKV, dtype=f32)
    decay_base = jnp.linspace(abs(math.log(1e-2)) / 1.5, abs(math.log(1e-2)) / 0.3, W_HYENA, dtype=f32)
    return {
        'x_prompt': nrm((BATCH, SEQ, D_MODEL)),
        'x_sample': nrm((DEC_BATCH, DEC_SEQ, D_MODEL)),
        'cache_k': nrm((DEC_BATCH, DEPTH, PAST_LEN, N_KV_HEADS, HEAD_DIM)),
        'cache_v': nrm((DEC_BATCH, DEPTH, PAST_LEN, N_KV_HEADS, HEAD_DIM)),
        'state_rwkv': nrm((DEC_BATCH, DEPTH, 2, N_RWKV_HEADS, RWKV_HEAD, RWKV_HEAD), 0.5),
        'c': nrm((DEC_BATCH, D_MODEL)),
        'c_ctx': nrm((D_MODEL,)),
        'w_mod': nrm((DEPTH, D_MODEL, 6 * D_MODEL), 0.5 * D_MODEL ** -0.5),
        'b_mod': nrm((DEPTH, 6 * D_MODEL), 0.02),
        'w_in': nrm((DEPTH, D_MODEL, IN_COLS), D_MODEL ** -0.5),
        'rwkv_shift': taps3(0.2, 0.6, RWKV_COLS, 0.05),
        'rwkv_w0': w0_base + nrm((DEPTH, 2, W_RWKV), 0.1),
        'rwkv_w2': nrm((DEPTH, 2, LORA_W, W_RWKV), 0.1),
        'rwkv_a0': nrm((DEPTH, 2, W_RWKV), 0.5),
        'rwkv_a2': nrm((DEPTH, 2, LORA_A, W_RWKV), 0.1),
        'rwkv_kk': 0.85 + nrm((DEPTH, W_RWKV), 0.05),
        'rwkv_ka': 1.0 + nrm((DEPTH, W_RWKV), 0.05),
        'rwkv_rk': nrm((DEPTH, W_RWKV), 0.1),
        'rwkv_g2': nrm((DEPTH, LORA_G, W_RWKV), LORA_G ** -0.5),
        'rwkv_gn_g': 1.0 + nrm((DEPTH, W_RWKV), 0.02),
        'rwkv_gn_b': nrm((DEPTH, W_RWKV), 0.02),
        'hy_short': taps3(0.25, 1.0, HYENA_COLS, 0.1),
        'hy_w1': nrm((DEPTH, HYENA_EMB, HYENA_FFN), HYENA_EMB ** -0.5),
        'hy_b1': nrm((DEPTH, HYENA_FFN), 0.1),
        'hy_freq': 1.0 + nrm((DEPTH, 2, HYENA_FFN), 0.1),
        'hy_w2': nrm((DEPTH, HYENA_FFN, HYENA_FFN), HYENA_FFN ** -0.5),
        'hy_b2': nrm((DEPTH, HYENA_FFN), 0.1),
        'hy_w3': nrm((DEPTH, HYENA_FFN, 2 * W_HYENA), 0.05 * HYENA_FFN ** -0.5),
        'hy_decay': decay_base + nrm((DEPTH, 2, W_HYENA), 0.1),
        'hy_bias': nrm((DEPTH, W_HYENA), 0.5),
        'attn_qn': 1.0 + nrm((DEPTH, HEAD_DIM), 0.02),
        'attn_kn': 1.0 + nrm((DEPTH, HEAD_DIM), 0.02),
        'w_out': nrm((DEPTH, D_MODEL, D_MODEL), BETA * D_MODEL ** -0.5),
        'ln1_g': 1.0 + nrm((DEPTH, D_MODEL), 0.02),
        'ln1_b': nrm((DEPTH, D_MODEL), 0.02),
        'ln2_g': 1.0 + nrm((DEPTH, D_MODEL), 0.02),
        'ln2_b': nrm((DEPTH, D_MODEL), 0.02),
        'ffn_up': nrm((DEPTH, D_MODEL, 2 * D_FF), D_MODEL ** -0.5),
        'ffn_conv': taps3(0.2, 1.0, 2 * D_FF, 0.05),
        'ffn_down': nrm((DEPTH, D_FF, D_MODEL), BETA * D_FF ** -0.5),
    }


def reference(x_prompt, x_sample, cache_k, cache_v, state_rwkv, c, c_ctx,
              w_mod, b_mod, w_in, rwkv_shift, rwkv_w0, rwkv_w2, rwkv_a0, rwkv_a2,
              rwkv_kk, rwkv_ka, rwkv_rk, rwkv_g2, rwkv_gn_g, rwkv_gn_b,
              hy_short, hy_w1, hy_b1, hy_freq, hy_w2, hy_b2, hy_w3, hy_decay, hy_bias,
              attn_qn, attn_kn, w_out, ln1_g, ln1_b, ln2_g, ln2_b,
              ffn_up, ffn_conv, ffn_down):
    n_ctx_req = x_prompt.shape[0]
    zero_state = jnp.zeros((n_ctx_req, N_RWKV_HEADS, RWKV_HEAD, RWKV_HEAD), jnp.float32)
    rope = axial_rope(x_sample.shape[1])
    ctx_cond = c_ctx[None, :]
    xp = x_prompt
    xs = x_sample
    new_k, new_v, new_s = [], [], []
    for l in range(DEPTH):
        lp = {
            'w_mod': w_mod[l], 'b_mod': b_mod[l], 'w_in': w_in[l],
            'rwkv_shift': rwkv_shift[l], 'rwkv_w0': rwkv_w0[l], 'rwkv_w2': rwkv_w2[l],
            'rwkv_a0': rwkv_a0[l], 'rwkv_a2': rwkv_a2[l], 'rwkv_kk': rwkv_kk[l],
            'rwkv_ka': rwkv_ka[l], 'rwkv_rk': rwkv_rk[l], 'rwkv_g2': rwkv_g2[l],
            'rwkv_gn_g': rwkv_gn_g[l], 'rwkv_gn_b': rwkv_gn_b[l],
            'hy_short': hy_short[l], 'hy_w1': hy_w1[l], 'hy_b1': hy_b1[l],
            'hy_freq': hy_freq[l], 'hy_w2': hy_w2[l], 'hy_b2': hy_b2[l],
            'hy_w3': hy_w3[l], 'hy_decay': hy_decay[l], 'hy_bias': hy_bias[l],
            'attn_qn': attn_qn[l], 'attn_kn': attn_kn[l], 'w_out': w_out[l],
            'ln1_g': ln1_g[l], 'ln1_b': ln1_b[l], 'ln2_g': ln2_g[l], 'ln2_b': ln2_b[l],
            'ffn_up': ffn_up[l], 'ffn_conv': ffn_conv[l], 'ffn_down': ffn_down[l],
        }
        xp, k_c, v_c, s_f, s_b = trunk_layer(xp, ctx_cond, lp, None, None, None, zero_state, zero_state)
        new_k.append(k_c)
        new_v.append(v_c)
        new_s.append(jnp.stack([s_f, s_b], axis=1))
        xs, _, _, _, _ = trunk_layer(xs, c, lp, rope, cache_k[:, l], cache_v[:, l],
                                     state_rwkv[:, l, 0], state_rwkv[:, l, 1])
    new_cache_k = jnp.stack(new_k, axis=1)
    new_cache_v = jnp.stack(new_v, axis=1)
    new_state_rwkv = jnp.stack(new_s, axis=1)
    return (xp, xs, new_cache_k, new_cache_v, new_state_rwkv)
```

```python
import functools
import math

import jax
import jax.numpy as jnp
from jax import lax
from jax.experimental import pallas as pl
from jax.experimental.pallas import tpu as pltpu

F32 = jnp.float32
BF16 = jnp.bfloat16

HEAD = 64
LANES = 128
SUBLANES = 8
VMEM_LIMIT = 52 * 1024 * 1024
LN_EPS = 1e-5
QK_EPS = 1e-6
GN_EPS = 64e-5
ROPE_THETA = 10000.0
GRID_W = 64
SCAN_T = 16
FFN_HALO = 16


def _cparams(*sem):
    return pltpu.CompilerParams(dimension_semantics=sem, vmem_limit_bytes=VMEM_LIMIT)


def _dot(a, b):
    return jnp.dot(a, b, preferred_element_type=F32)


def _hi_f32(x):
    u = pltpu.bitcast(x, jnp.uint32) & jnp.uint32(0xFFFF0000)
    return pltpu.bitcast(u, F32)


def _split2(x):
    h = _hi_f32(x)
    return h.astype(BF16), (x - h).astype(BF16)


def _split3(x):
    h1 = _hi_f32(x)
    r1 = x - h1
    h2 = _hi_f32(r1)
    return h1.astype(BF16), h2.astype(BF16), (r1 - h2).astype(BF16)


def _dot3(a, b_hi, b_lo):
    a_hi, a_lo = _split2(a)
    return _dot(a_hi, b_hi) + (_dot(a_lo, b_hi) + _dot(a_hi, b_lo))


def _segsum(x, ones_bd):
    h1, h2, h3 = _split3(x)
    return _dot(h1, ones_bd) + (_dot(h2, ones_bd) + _dot(h3, ones_bd))


def _layer_norm(x, g, b):
    mu = jnp.mean(x, axis=-1, keepdims=True)
    xc = x - mu
    var = jnp.mean(xc * xc, axis=-1, keepdims=True)
    return xc * lax.rsqrt(var + LN_EPS) * g + b


def _sigmoid(x):
    return 1.0 / (1.0 + jnp.exp(-x))


def _conv3(x, prev_row, next_row, w, row0, seq_len):
    tl = x.shape[0]
    row = lax.broadcasted_iota(jnp.int32, (tl, 1), 0)
    pos = (row0 + row) % seq_len
    xm = jnp.where(row == 0, prev_row, pltpu.roll(x, 1, 0))
    xp = jnp.where(row == tl - 1, next_row, pltpu.roll(x, tl - 1, 0))
    xm = jnp.where(pos == 0, 0.0, xm)
    xp = jnp.where(pos == seq_len - 1, 0.0, xp)
    return xm * w[0:1, :] + x * w[1:2, :] + xp * w[2:3, :]


def _halo_specs(tl, width, n_rows, halo=SUBLANES):
    r = tl // halo
    last = n_rows // halo - 1
    prev = pl.BlockSpec((1, halo, width), lambda b, i: (b, jnp.maximum(i * r - 1, 0), 0))
    nxt = pl.BlockSpec((1, halo, width), lambda b, i: (b, jnp.minimum((i + 1) * r, last), 0))
    return prev, nxt


def _bm(arr):
    if arr.shape[0] == 1:
        return lambda b: 0
    return lambda b: b


def _mod_kernel(c_ref, w_ref, b_ref, o_ref):
    c = c_ref[...]
    s = c * _sigmoid(c)
    w_hi, w_lo = _split2(w_ref[0])
    o_ref[0] = _dot3(s, w_hi, w_lo) + b_ref[0]


def _modulation(cond, w_mod, b_mod):
    depth, d, n = w_mod.shape
    rows = cond.shape[0]
    tn = 1024
    return pl.pallas_call(
        _mod_kernel,
        grid=(depth, n // tn),
        in_specs=[
            pl.BlockSpec((rows, d), lambda l, j: (0, 0)),
            pl.BlockSpec((1, d, tn), lambda l, j: (l, 0, j)),
            pl.BlockSpec((1, 1, tn), lambda l, j: (l, 0, j)),
        ],
        out_specs=pl.BlockSpec((1, rows, tn), lambda l, j: (l, 0, j)),
        out_shape=jax.ShapeDtypeStruct((depth, rows, n), F32),
        compiler_params=_cparams("parallel", "parallel"),
        name="modulation",
    )(cond, w_mod, b_mod.reshape(depth, 1, n))


def _proj_kernel(x_ref, sc_ref, sh_ref, w_ref, o_ref):
    h = (x_ref[0] * (1.0 + sc_ref[0]) + sh_ref[0]).astype(BF16)
    o_ref[0] = _dot(h, w_ref[...])


def _mod_proj(x, sc, sh, w, tl):
    bx, lx, d = x.shape
    n = w.shape[1]
    sel = _bm(sc)
    return pl.pallas_call(
        _proj_kernel,
        grid=(bx, lx // tl),
        in_specs=[
            pl.BlockSpec((1, tl, d), lambda b, i: (b, i, 0)),
            pl.BlockSpec((1, 1, d), lambda b, i: (sel(b), 0, 0)),
            pl.BlockSpec((1, 1, d), lambda b, i: (sel(b), 0, 0)),
            pl.BlockSpec((d, n), lambda b, i: (0, 0)),
        ],
        out_specs=pl.BlockSpec((1, tl, n), lambda b, i: (b, i, 0)),
        out_shape=jax.ShapeDtypeStruct((bx, lx, n), F32),
        compiler_params=_cparams("parallel", "parallel"),
        name="mod_proj",
    )(x, sc, sh, w)


def _store_heads(o_ref, x):
    for h in range(x.shape[1] // HEAD):
        o_ref[0, h] = x[:, HEAD * h:HEAD * (h + 1)].astype(o_ref.dtype)


def _rwkv_prep_kernel(p_ref, pp_ref, pn_ref, sw_ref, w0_ref, a0_ref, kkp_ref, ka_ref, rk_ref,
                      w2h_ref, w2l_ref, a2h_ref, a2l_ref, g2h_ref, g2l_ref, ones_ref,
                      r_o, v_o, kk_o, wf_o, wb_o, kdf_o, kdb_o, bf_o, bb_o, bonus_o, g_o,
                      *, seq_len, width):
    tl = p_ref.shape[1]
    i = pl.program_id(1)
    pc = _conv3(p_ref[0], pp_ref[0, SUBLANES - 1:SUBLANES, :], pn_ref[0, 0:1, :],
                sw_ref[...], i * tl, seq_len)
    w = width
    r = pc[:, 0:w]
    k = pc[:, w:2 * w]
    v = pc[:, 2 * w:3 * w]
    wd = pc[:, 3 * w:3 * w + LANES]
    ad = pc[:, 3 * w + LANES:3 * w + 2 * LANES]
    gd = pc[:, 3 * w + 2 * LANES:3 * w + 3 * LANES]
    ones_bd = ones_ref[...]

    g_o[0] = _dot3(_sigmoid(gd), g2h_ref[...], g2l_ref[...])
    kkr = k * kkp_ref[...]
    kk = kkr * lax.rsqrt(_segsum(kkr * kkr, ones_bd) + 1e-12)
    lw = _dot3(jnp.tanh(wd), w2h_ref[...], w2l_ref[...])
    la = _dot3(ad, a2h_ref[...], a2l_ref[...])

    _store_heads(r_o, r)
    _store_heads(v_o, v)
    _store_heads(kk_o, kk)
    rrk = r * rk_ref[...]
    bonus = jnp.zeros_like(r)
    for d, (w_o, kd_o, b_o) in enumerate(((wf_o, kdf_o, bf_o), (wb_o, kdb_o, bb_o))):
        z = w0_ref[d:d + 1, :] + lw[:, d * w:(d + 1) * w]
        w_log = -(jnp.maximum(-z, 0.0) + jnp.log(1.0 + jnp.exp(-jnp.abs(z)))) - 0.5
        decay = jnp.exp(-jnp.exp(w_log))
        a = _sigmoid(a0_ref[d:d + 1, :] + la[:, d * w:(d + 1) * w])
        kd = k * (1.0 + (a - 1.0) * ka_ref[...])
        _store_heads(w_o, decay)
        _store_heads(kd_o, kd)
        _store_heads(b_o, kk * a)
        bonus = bonus + _segsum(rrk * kd, ones_bd) * v
    bonus_o[0] = bonus


def _rwkv_prep(p_r, lp, tl, seq_len):
    b, l, wp = p_r.shape
    w = lp["rwkv_w"]
    nh = w // HEAD
    hm = jax.ShapeDtypeStruct((b, nh, l, HEAD), F32)
    tok = jax.ShapeDtypeStruct((b, l, w), F32)
    prev, nxt = _halo_specs(tl, wp, l)

    def full(a):
        return pl.BlockSpec(a.shape, lambda b_, i: (0,) * a.ndim)

    consts = [lp["rwkv_shift"], lp["rwkv_w0"], lp["rwkv_a0"], lp["rwkv_kk"], lp["rwkv_ka"],
              lp["rwkv_rk"], lp["w2_hi"], lp["w2_lo"], lp["a2_hi"], lp["a2_lo"],
              lp["g2_hi"], lp["g2_lo"], lp["ones_head"]]
    hm_spec = pl.BlockSpec((1, nh, tl, HEAD), lambda b_, i: (b_, 0, i, 0))
    tok_spec = pl.BlockSpec((1, tl, w), lambda b_, i: (b_, i, 0))
    return pl.pallas_call(
        functools.partial(_rwkv_prep_kernel, seq_len=seq_len, width=w),
        grid=(b, l // tl),
        in_specs=[pl.BlockSpec((1, tl, wp), lambda b_, i: (b_, i, 0)), prev, nxt]
        + [full(a) for a in consts],
        out_specs=[hm_spec] * 9 + [tok_spec] * 2,
        out_shape=[hm] * 9 + [tok] * 2,
        compiler_params=_cparams("parallel", "parallel"),
        name="rwkv_prep",
    )(p_r, p_r, p_r, *consts)


def _allsum8(x):
    s = x[0:8]
    for q in range(1, x.shape[0] // SUBLANES):
        s = s + x[q * SUBLANES:(q + 1) * SUBLANES]
    s = s + pltpu.roll(s, 4, 0)
    s = s + pltpu.roll(s, 2, 0)
    return s + pltpu.roll(s, 1, 0)


def _wkv_kernel(rf, rb, vf, vb, kkf, kkb, wf, wb, kdf, kdb, bf, bb, s0_ref,
                yf_ref, yb_ref, so_ref,
                S, R, W, KD, V, KK, BB, YF, YB):
    i = pl.program_id(1)
    n = pl.num_programs(1)
    t_steps = R.shape[0] // HEAD

    @pl.when(i == 0)
    def _():
        S[...] = jnp.concatenate([s0_ref[0], s0_ref[1]], axis=0).T

    for dst, f_ref, b_ref in ((R, rf, rb), (V, vf, vb), (KK, kkf, kkb),
                              (W, wf, wb), (KD, kdf, kdb), (BB, bf, bb)):
        dst[...] = jnp.concatenate([f_ref[...], b_ref[...]], axis=0).T

    is_f = lax.broadcasted_iota(jnp.int32, (HEAD, LANES), 1) < HEAD
    is_f8 = is_f[0:SUBLANES]
    sub = lax.broadcasted_iota(jnp.int32, (SUBLANES, LANES), 0)
    nq = HEAD // SUBLANES

    def step(j, carry):
        jf = pl.multiple_of(j * HEAD, HEAD)
        jb = pl.multiple_of((t_steps - 1 - j) * HEAD, HEAD)

        def ld(buf):
            return jnp.where(is_f, buf[pl.ds(jf, HEAD), :], buf[pl.ds(jb, HEAD), :])

        r, w, kd, vv, kk, b = ld(R), ld(W), ld(KD), ld(V), ld(KK), ld(BB)
        wr = w * r
        br = _allsum8(b * r)
        kr = _allsum8(kd * r)
        ytiles = [jnp.zeros((SUBLANES, LANES), F32) for _ in range(nq)]
        for vi in range(HEAD):
            sv = S[vi * HEAD:(vi + 1) * HEAD, :]
            sa = _allsum8(sv * kk)
            y0 = _allsum8(sv * wr)
            vrow = jnp.broadcast_to(vv[vi:vi + 1, :], (SUBLANES, LANES))
            for q in range(nq):
                lo, hi = q * SUBLANES, (q + 1) * SUBLANES
                S[vi * HEAD + lo:vi * HEAD + hi, :] = (
                    sv[lo:hi] * w[lo:hi] - sa * b[lo:hi] + vrow * kd[lo:hi])
            yv = y0 - sa * br + vrow * kr
            ytiles[vi // SUBLANES] = jnp.where(sub == vi % SUBLANES, yv, ytiles[vi // SUBLANES])
        y = jnp.concatenate(ytiles, axis=0)
        YF[pl.ds(jf, HEAD), :] = y
        YB[pl.ds(jb, HEAD), :] = y
        return carry

    lax.fori_loop(0, t_steps, step, 0)

    is_ft = lax.broadcasted_iota(jnp.int32, YF.shape, 1) < HEAD
    yt = jnp.where(is_ft, YF[...], YB[...]).T
    yf_ref[...] = yt[0:HEAD]
    yb_ref[...] = yt[HEAD:2 * HEAD]

    @pl.when(i == n - 1)
    def _():
        st = S[...].T
        so_ref[0] = st[0:HEAD]
        so_ref[1] = st[HEAD:2 * HEAD]


def _wkv_scan(r, v, kk, wf, wb, kdf, kdb, bf, bb, s0):
    nc, lh = r.shape
    groups = nc // HEAD
    tc = SCAN_T * HEAD
    n = lh // tc
    fwd = pl.BlockSpec((HEAD, tc), lambda g, i: (g, i))
    bwd = pl.BlockSpec((HEAD, tc), lambda g, i: (g, n - 1 - i))
    st = pl.BlockSpec((2, HEAD, HEAD * HEAD), lambda g, i: (0, g, 0))
    buf = pltpu.VMEM((tc, LANES), F32)
    return pl.pallas_call(
        _wkv_kernel,
        grid=(groups, n),
        in_specs=[fwd, bwd, fwd, bwd, fwd, bwd, fwd, bwd, fwd, bwd, fwd, bwd, st],
        out_specs=[fwd, bwd, st],
        out_shape=[jax.ShapeDtypeStruct((nc, lh), F32), jax.ShapeDtypeStruct((nc, lh), F32),
                   jax.ShapeDtypeStruct((2, nc, HEAD * HEAD), F32)],
        scratch_shapes=[pltpu.VMEM((HEAD * HEAD, LANES), F32)] + [buf] * 8,
        compiler_params=_cparams("parallel", "arbitrary"),
        name="wkv_scan",
    )(r, r, v, v, kk, kk, wf, wb, kdf, kdb, bf, bb, s0)


def _rwkv_post_kernel(yf_ref, yb_ref, bonus_ref, g_ref, gng_ref, gnb_ref, o_ref):
    y = yf_ref[0] + yb_ref[0]
    mu = jnp.mean(y, axis=-1, keepdims=True)
    yc = y - mu
    var = jnp.mean(yc * yc, axis=-1, keepdims=True)
    yn = yc * lax.rsqrt(var + GN_EPS)
    yt = jnp.concatenate([yn[h] for h in range(yn.shape[0])], axis=-1)
    o_ref[0] = (yt * gng_ref[...] + gnb_ref[...] + bonus_ref[0]) * g_ref[0]


def _rwkv_post(yf, yb, bonus, g, gn_g, gn_b, tl):
    b, nh, l, _ = yf.shape
    w = nh * HEAD
    hm = pl.BlockSpec((1, nh, tl, HEAD), lambda b_, i: (b_, 0, i, 0))
    tok = pl.BlockSpec((1, tl, w), lambda b_, i: (b_, i, 0))
    vec = pl.BlockSpec((1, w), lambda b_, i: (0, 0))
    return pl.pallas_call(
        _rwkv_post_kernel,
        grid=(b, l // tl),
        in_specs=[hm, hm, tok, tok, vec, vec],
        out_specs=tok,
        out_shape=jax.ShapeDtypeStruct((b, l, w), F32),
        compiler_params=_cparams("parallel", "parallel"),
        name="rwkv_post",
    )(yf, yb, bonus, g, gn_g, gn_b)


def _hy_prep_kernel(p_ref, pp_ref, pn_ref, sw_ref, x0_o, uh_o, ul_o, *, seq_len, width):
    tl = p_ref.shape[1]
    i = pl.program_id(1)
    pc = _conv3(p_ref[0], pp_ref[0, SUBLANES - 1:SUBLANES, :], pn_ref[0, 0:1, :],
                sw_ref[...], i * tl, seq_len)
    w = width
    x0_o[0] = pc[:, 0:w]
    u = pc[:, w:2 * w] * pc[:, 2 * w:3 * w]
    hi, lo = _split2(u)
    uh_o[0] = hi
    ul_o[0] = lo


def _hy_prep(p_h, short_w, tl, seq_len):
    b, l, w3 = p_h.shape
    w = w3 // 3
    prev, nxt = _halo_specs(tl, w3, l)
    tok = pl.BlockSpec((1, tl, w), lambda b_, i: (b_, i, 0))
    return pl.pallas_call(
        functools.partial(_hy_prep_kernel, seq_len=seq_len, width=w),
        grid=(b, l // tl),
        in_specs=[pl.BlockSpec((1, tl, w3), lambda b_, i: (b_, i, 0)), prev, nxt,
                  pl.BlockSpec((3, w3), lambda b_, i: (0, 0))],
        out_specs=[tok, tok, tok],
        out_shape=[jax.ShapeDtypeStruct((b, l, w), F32), jax.ShapeDtypeStruct((b, l, w), BF16),
                   jax.ShapeDtypeStruct((b, l, w), BF16)],
        compiler_params=_cparams("parallel", "parallel"),
        name="hy_prep",
    )(p_h, p_h, p_h, short_w)


def _hy_mlp_kernel(z_ref, w1_ref, b1_ref, f_ref, w2_ref, b2_ref, w3_ref, dec_ref, hh_o, hl_o):
    z = z_ref[...]
    t01 = z[:, 0:1]
    w1h, w1l = _split2(w1_ref[...])
    w2h, w2l = _split2(w2_ref[...])
    w3h, w3l = _split2(w3_ref[...])
    h = jnp.sin(f_ref[0:1, :] * (_dot3(z, w1h, w1l) + b1_ref[...]))
    h = jnp.sin(f_ref[1:2, :] * (_dot3(h, w2h, w2l) + b2_ref[...]))
    h = _dot3(h, w3h, w3l) * jnp.exp(-t01 * jnp.abs(dec_ref[...]))
    hi, lo = _split2(h)
    hh_o[0] = hi
    hl_o[0] = lo


def _hy_mlp(z, lp, tl):
    n = z.shape[0]
    c2 = lp["hy_w3"].shape[1]

    def full(a):
        return pl.BlockSpec(a.shape, lambda i: (0,) * a.ndim)

    consts = [lp["hy_w1"], lp["hy_b1"], lp["hy_freq"], lp["hy_w2"], lp["hy_b2"], lp["hy_w3"],
              lp["hy_decay"]]
    out = pl.BlockSpec((1, tl, c2), lambda i: (0, i, 0))
    return pl.pallas_call(
        _hy_mlp_kernel,
        grid=(n // tl,),
        in_specs=[pl.BlockSpec((tl, LANES), lambda i: (i, 0))] + [full(a) for a in consts],
        out_specs=[out, out],
        out_shape=[jax.ShapeDtypeStruct((1, n, c2), BF16)] * 2,
        compiler_params=_cparams("parallel"),
        name="hy_mlp",
    )(z, *consts)


def _dft3(fh_ref, fl_ref, xh, xl):
    fh = fh_ref[...]
    return _dot(fh, xh) + (_dot(fl_ref[...], xh) + _dot(fh, xl))


def _dft_filter_kernel(frh, frl, fih, fil, xh_ref, xl_ref, hre_o, him_o):
    xh, xl = xh_ref[0], xl_ref[0]
    w = xh.shape[1] // 2
    xre = _dft3(frh, frl, xh, xl)
    xim = _dft3(fih, fil, xh, xl)
    hre_o[...] = xre[:, :w] + xre[:, w:]
    row = lax.broadcasted_iota(jnp.int32, (xre.shape[0], 1), 0) + pl.program_id(0) * xre.shape[0]
    him_o[...] = jnp.where(row == 0, xim[:, :w] + xim[:, w:], xim[:, :w] - xim[:, w:])


def _dft_filter(mats, hh, hl, tf):
    n = hh.shape[1]
    w = hh.shape[2] // 2
    ft = pl.BlockSpec((tf, n), lambda i: (i, 0))
    xs = pl.BlockSpec((1, n, 2 * w), lambda i: (0, 0, 0))
    out = pl.BlockSpec((tf, w), lambda i: (i, 0))
    return pl.pallas_call(
        _dft_filter_kernel,
        grid=(n // tf,),
        in_specs=[ft, ft, ft, ft, xs, xs],
        out_specs=[out, out],
        out_shape=[jax.ShapeDtypeStruct((n, w), F32)] * 2,
        compiler_params=_cparams("parallel"),
        name="dft_filter",
    )(mats["fre_hi"], mats["fre_lo"], mats["fim_hi"], mats["fim_lo"], hh, hl)


def _dft_signal_kernel(frh, frl, fih, fil, xh_ref, xl_ref, hre_ref, him_ref,
                       prh_o, prl_o, pih_o, pil_o):
    xh, xl = xh_ref[0], xl_ref[0]
    xre = _dft3(frh, frl, xh, xl)
    xim = _dft3(fih, fil, xh, xl)
    hre, him = hre_ref[...], him_ref[...]
    row = lax.broadcasted_iota(jnp.int32, (xre.shape[0], 1), 0) + pl.program_id(1) * xre.shape[0]
    pre = jnp.where(row == 0, xre * hre, xre * hre - xim * him)
    pim = jnp.where(row == 0, xim * him, xre * him + xim * hre)
    prh_o[0], prl_o[0] = _split2(pre)
    pih_o[0], pil_o[0] = _split2(pim)


def _dft_signal(mats, uh, ul, hre, him, tf):
    b, n, w = uh.shape
    ft = pl.BlockSpec((tf, n), lambda b_, i: (i, 0))
    xs = pl.BlockSpec((1, n, w), lambda b_, i: (b_, 0, 0))
    hs = pl.BlockSpec((tf, w), lambda b_, i: (i, 0))
    out = pl.BlockSpec((1, tf, w), lambda b_, i: (b_, i, 0))
    return pl.pallas_call(
        _dft_signal_kernel,
        grid=(b, n // tf),
        in_specs=[ft, ft, ft, ft, xs, xs, hs, hs],
        out_specs=[out] * 4,
        out_shape=[jax.ShapeDtypeStruct((b, n, w), BF16)] * 4,
        compiler_params=_cparams("parallel", "parallel"),
        name="dft_signal",
    )(mats["fre_hi"], mats["fre_lo"], mats["fim_hi"], mats["fim_lo"], uh, ul, hre, him)


def _dft_inverse_kernel(grh, grl, gih, gil, prh, prl, pih, pil, x0_ref, uh_ref, ul_ref, bias_ref,
                        o_ref):
    y = _dft3(grh, grl, prh[0], prl[0]) + _dft3(gih, gil, pih[0], pil[0])
    u = uh_ref[0].astype(F32) + ul_ref[0].astype(F32)
    o_ref[0] = x0_ref[0] * (y + u * bias_ref[...])


def _dft_inverse(mats, p4, x0, uh, ul, bias, tt):
    b, n, w = uh.shape
    gt = pl.BlockSpec((tt, n), lambda b_, i: (i, 0))
    ps = pl.BlockSpec((1, n, w), lambda b_, i: (b_, 0, 0))
    tok = pl.BlockSpec((1, tt, w), lambda b_, i: (b_, i, 0))
    return pl.pallas_call(
        _dft_inverse_kernel,
        grid=(b, n // tt),
        in_specs=[gt, gt, gt, gt, ps, ps, ps, ps, tok, tok, tok,
                  pl.BlockSpec((1, w), lambda b_, i: (0, 0))],
        out_specs=tok,
        out_shape=jax.ShapeDtypeStruct((b, n, w), F32),
        compiler_params=_cparams("parallel", "parallel"),
        name="dft_inverse",
    )(mats["gre_hi"], mats["gre_lo"], mats["gim_hi"], mats["gim_lo"], *p4, x0, uh, ul, bias)


def _swap_halves(x):
    lane = lax.broadcasted_iota(jnp.int32, x.shape, 1)
    half = HEAD // 2
    return jnp.where(lane % HEAD < half, pltpu.roll(x, LANES - half, 1), pltpu.roll(x, half, 1))


def _attn_prep_kernel(*refs, rope, wq, wkv):
    if rope:
        p_ref, qn_ref, kn_ref, ones_ref, cos_ref, sin_ref, q_o, k_o, v_o, kt_o = refs
    else:
        p_ref, qn_ref, kn_ref, ones_ref, q_o, k_o, v_o, kt_o = refs
    ones_bd = ones_ref[...]

    def norm_rope(x, gain):
        ms = _segsum(x * x, ones_bd) * (1.0 / HEAD)
        xn = x * lax.rsqrt(ms + QK_EPS) * gain
        if rope:
            xn = xn * cos_ref[...] + _swap_halves(xn) * sin_ref[...]
        return xn

    scale = HEAD ** -0.5
    for c in range(wq // LANES):
        xq = norm_rope(p_ref[0, :, c * LANES:(c + 1) * LANES], qn_ref[...]) * scale
        q_o[0, 2 * c] = xq[:, :HEAD].astype(BF16)
        q_o[0, 2 * c + 1] = xq[:, HEAD:].astype(BF16)
    for c in range(wkv // LANES):
        x = p_ref[0, :, wq + c * LANES:wq + (c + 1) * LANES]
        ms = _segsum(x * x, ones_bd) * (1.0 / HEAD)
        xk = x * lax.rsqrt(ms + QK_EPS) * kn_ref[...]
        kt_o[0, :, c * LANES:(c + 1) * LANES] = xk
        if rope:
            xk = xk * cos_ref[...] + _swap_halves(xk) * sin_ref[...]
        k_o[0, 2 * c] = xk[:, :HEAD].astype(BF16)
        k_o[0, 2 * c + 1] = xk[:, HEAD:].astype(BF16)
        xv = p_ref[0, :, wq + wkv + c * LANES:wq + wkv + (c + 1) * LANES]
        v_o[0, 2 * c] = xv[:, :HEAD].astype(BF16)
        v_o[0, 2 * c + 1] = xv[:, HEAD:].astype(BF16)


def _attn_prep(p_a, lp, rope_tabs, tl, wq, wkv):
    b, l, wa = p_a.shape
    nq, nkv = wq // HEAD, wkv // HEAD
    rope = rope_tabs is not None
    vec = pl.BlockSpec((1, LANES), lambda b_, i: (0, 0))
    in_specs = [pl.BlockSpec((1, tl, wa), lambda b_, i: (b_, i, 0)), vec, vec,
                pl.BlockSpec((LANES, LANES), lambda b_, i: (0, 0))]
    args = [p_a, lp["attn_qn"], lp["attn_kn"], lp["ones_pair"]]
    if rope:
        tab = pl.BlockSpec((tl, LANES), lambda b_, i: (i, 0))
        in_specs += [tab, tab]
        args += list(rope_tabs)
    return pl.pallas_call(
        functools.partial(_attn_prep_kernel, rope=rope, wq=wq, wkv=wkv),
        grid=(b, l // tl),
        in_specs=in_specs,
        out_specs=[pl.BlockSpec((1, nq, tl, HEAD), lambda b_, i: (b_, 0, i, 0)),
                   pl.BlockSpec((1, nkv, tl, HEAD), lambda b_, i: (b_, 0, i, 0)),
                   pl.BlockSpec((1, nkv, tl, HEAD), lambda b_, i: (b_, 0, i, 0)),
                   pl.BlockSpec((1, tl, wkv), lambda b_, i: (b_, i, 0))],
        out_shape=[jax.ShapeDtypeStruct((b, nq, l, HEAD), BF16),
                   jax.ShapeDtypeStruct((b, nkv, l, HEAD), BF16),
                   jax.ShapeDtypeStruct((b, nkv, l, HEAD), BF16),
                   jax.ShapeDtypeStruct((b, l, wkv), F32)],
        compiler_params=_cparams("parallel", "parallel"),
        name="attn_prep",
    )(*args)


def _attn_kernel(q_ref, k_ref, v_ref, o_ref):
    g, tq, _ = q_ref.shape[1:]
    q = q_ref[0].reshape(g * tq, HEAD)
    s = lax.dot_general(q, k_ref[0, 0], (((1,), (1,)), ((), ())), preferred_element_type=F32)
    m = jnp.max(s, axis=-1, keepdims=True)
    p = jnp.exp(s - m)
    den = jnp.sum(p, axis=-1, keepdims=True)
    o = _dot(p.astype(BF16), v_ref[0, 0]) / den
    o_ref[0] = jnp.concatenate([o[h * tq:(h + 1) * tq] for h in range(g)], axis=-1)


def _attention(q, k, v, tq):
    b, nq, l, _ = q.shape
    nkv, lk = k.shape[1], k.shape[2]
    g = nq // nkv
    kv = pl.BlockSpec((1, 1, lk, HEAD), lambda b_, h, i: (b_, h, 0, 0))
    return pl.pallas_call(
        _attn_kernel,
        grid=(b, nkv, l // tq),
        in_specs=[pl.BlockSpec((1, g, tq, HEAD), lambda b_, h, i: (b_, h, i, 0)), kv, kv],
        out_specs=pl.BlockSpec((1, tq, g * HEAD), lambda b_, h, i: (b_, i, h)),
        out_shape=jax.ShapeDtypeStruct((b, l, nq * HEAD), F32),
        compiler_params=_cparams("parallel", "parallel", "parallel"),
        name="attention",
    )(q, k, v)


def _out_proj_kernel(yr_ref, yh_ref, ya_ref, x_ref, g_ref, w_ref, lng_ref, lnb_ref, o_ref, *, alpha):
    wr, wh = yr_ref.shape[2], yh_ref.shape[2]
    mix = _dot(yr_ref[0].astype(BF16), w_ref[0:wr, :])
    mix = mix + _dot(yh_ref[0].astype(BF16), w_ref[wr:wr + wh, :])
    mix = mix + _dot(ya_ref[0].astype(BF16), w_ref[wr + wh:, :])
    o_ref[0] = _layer_norm(alpha * x_ref[0] + g_ref[0] * mix, lng_ref[...], lnb_ref[...])


def _out_proj(y_r, y_h, y_a, x, gate, w_out, ln_g, ln_b, tl, alpha):
    bx, lx, d = x.shape
    sel = _bm(gate)

    def tok(a):
        return pl.BlockSpec((1, tl, a.shape[2]), lambda b, i: (b, i, 0))

    vec = pl.BlockSpec((1, d), lambda b, i: (0, 0))
    return pl.pallas_call(
        functools.partial(_out_proj_kernel, alpha=alpha),
        grid=(bx, lx // tl),
        in_specs=[tok(y_r), tok(y_h), tok(y_a), tok(x),
                  pl.BlockSpec((1, 1, d), lambda b, i: (sel(b), 0, 0)),
                  pl.BlockSpec(w_out.shape, lambda b, i: (0, 0)), vec, vec],
        out_specs=tok(x),
        out_shape=jax.ShapeDtypeStruct((bx, lx, d), F32),
        compiler_params=_cparams("parallel", "parallel"),
        name="out_proj",
    )(y_r, y_h, y_a, x, gate, w_out, ln_g, ln_b)


def _ffn_kernel(x_ref, xp_ref, xn_ref, sc_ref, sh_ref, g_ref, wa_ref, wb_ref, ca_ref, cb_ref,
                wd_ref, lng_ref, lnb_ref, o_ref, h_s, acc_s, *, seq_len, alpha):
    tl = x_ref.shape[1]
    i = pl.program_id(1)
    j = pl.program_id(2)
    halo = FFN_HALO

    @pl.when(j == 0)
    def _():
        sc, sh = 1.0 + sc_ref[0], sh_ref[0]
        h_s[0:halo] = (xp_ref[0] * sc + sh).astype(BF16)
        h_s[halo:halo + tl] = (x_ref[0] * sc + sh).astype(BF16)
        h_s[halo + tl:] = (xn_ref[0] * sc + sh).astype(BF16)
        acc_s[...] = jnp.zeros_like(acc_s)

    row = lax.broadcasted_iota(jnp.int32, (tl, 1), 0)
    pos = (i * tl + row) % seq_len
    first = pos == 0
    last = pos == seq_len - 1
    h = h_s[...]

    def conv_up(w_ref, c_ref):
        u = _dot(h, w_ref[...])
        um = jnp.where(first, 0.0, pltpu.roll(u, 1, 0)[halo:halo + tl])
        up = jnp.where(last, 0.0, pltpu.roll(u, tl + 2 * halo - 1, 0)[halo:halo + tl])
        return um * c_ref[0:1, :] + u[halo:halo + tl] * c_ref[1:2, :] + up * c_ref[2:3, :]

    a = conv_up(wa_ref, ca_ref)
    b = conv_up(wb_ref, cb_ref)
    f = (a * _sigmoid(a) * b).astype(BF16)
    acc_s[...] += _dot(f, wd_ref[...])

    @pl.when(j == pl.num_programs(2) - 1)
    def _():
        o_ref[0] = _layer_norm(alpha * x_ref[0] + g_ref[0] * acc_s[...], lng_ref[...], lnb_ref[...])


def _ffn(x, sc, sh, gate, w_up, conv_w, w_down, ln_g, ln_b, tl, tn, seq_len, alpha):
    bx, lx, d = x.shape
    dff = w_down.shape[0]
    nj = dff // tn
    sel = _bm(sc)
    prev, nxt = _halo_specs(tl, d, lx, FFN_HALO)
    prev3 = pl.BlockSpec(prev.block_shape, lambda b, i, j: prev.index_map(b, i))
    nxt3 = pl.BlockSpec(nxt.block_shape, lambda b, i, j: nxt.index_map(b, i))
    mod = pl.BlockSpec((1, 1, d), lambda b, i, j: (sel(b), 0, 0))
    vec = pl.BlockSpec((1, d), lambda b, i, j: (0, 0))
    tok = pl.BlockSpec((1, tl, d), lambda b, i, j: (b, i, 0))
    return pl.pallas_call(
        functools.partial(_ffn_kernel, seq_len=seq_len, alpha=alpha),
        grid=(bx, lx // tl, nj),
        in_specs=[tok, prev3, nxt3, mod, mod, mod,
                  pl.BlockSpec((d, tn), lambda b, i, j: (0, j)),
                  pl.BlockSpec((d, tn), lambda b, i, j: (0, j + nj)),
                  pl.BlockSpec((3, tn), lambda b, i, j: (0, j)),
                  pl.BlockSpec((3, tn), lambda b, i, j: (0, j + nj)),
                  pl.BlockSpec((tn, d), lambda b, i, j: (j, 0)), vec, vec],
        out_specs=tok,
        out_shape=jax.ShapeDtypeStruct((bx, lx, d), F32),
        scratch_shapes=[pltpu.VMEM((tl + 2 * FFN_HALO, d), BF16), pltpu.VMEM((tl, d), F32)],
        compiler_params=_cparams("parallel", "parallel", "arbitrary"),
        name="conv_ffn",
    )(x, x, x, sc, sh, gate, w_up, w_up, conv_w, conv_w, w_down, ln_g, ln_b)


def _split2_host(x):
    hi = lax.bitcast_convert_type(lax.bitcast_convert_type(x, jnp.uint32) & jnp.uint32(0xFFFF0000), F32)
    return hi.astype(BF16), (x - hi).astype(BF16)


def _dft_mats(n):
    big = 2 * n
    k = jnp.arange(n, dtype=jnp.int32)[:, None]
    t = jnp.arange(n, dtype=jnp.int32)[None, :]
    ang = ((k * t) % big).astype(F32) * (2.0 * math.pi / big)
    fre = jnp.cos(ang)
    fim = jnp.where(k == 0, jnp.where(t % 2 == 0, 1.0, -1.0), -jnp.sin(ang))
    scale = jnp.where(k == 0, 1.0 / big, 2.0 / big)
    out = {}
    for name, m in (("fre", fre), ("fim", fim), ("gre", (fre * scale).T), ("gim", (fim * scale).T)):
        out[name + "_hi"], out[name + "_lo"] = _split2_host(m)
    return out


def _hyena_features(n, n_bands):
    t01 = jnp.linspace(0.0, 1.0, n, dtype=F32)[:, None]
    pos = jnp.arange(n, dtype=F32)[:, None]
    bands = jnp.linspace(1e-4, n_bands - 1, n_bands, dtype=F32)[None, :]
    ang = (2.0 * math.pi / n) * pos * bands
    z = jnp.concatenate([t01, jnp.cos(ang), -jnp.sin(ang)], -1)
    return jnp.pad(z, ((0, 0), (0, LANES - z.shape[1])))


def _rope_tables(n_tokens):
    rows = n_tokens // GRID_W
    row = jnp.repeat(jnp.arange(rows, dtype=F32), GRID_W)
    col = jnp.tile(jnp.arange(GRID_W, dtype=F32), rows)
    n_freq = HEAD // 4
    inv = ROPE_THETA ** (-jnp.arange(n_freq, dtype=F32) / n_freq)
    ang = jnp.concatenate([row[:, None] * inv, col[:, None] * inv], -1)
    cos, sin = jnp.cos(ang), jnp.sin(ang)
    cos2 = jnp.tile(jnp.concatenate([cos, cos], -1), (1, LANES // HEAD))
    sin2 = jnp.tile(jnp.concatenate([-sin, sin], -1), (1, LANES // HEAD))
    return cos2, sin2


def _block_ones(n, group):
    idx = jnp.arange(n) // group
    return (idx[:, None] == idx[None, :]).astype(BF16)


def _pad_to(a, shape):
    return jnp.pad(a, [(0, s - d) for d, s in zip(a.shape, shape)])


def _block_diag2(a, b):
    za = jnp.zeros((a.shape[0], b.shape[1]), a.dtype)
    zb = jnp.zeros((b.shape[0], a.shape[1]), a.dtype)
    return jnp.concatenate([jnp.concatenate([a, za], 1), jnp.concatenate([zb, b], 1)], 0)


def _layer_params(l, P, dims):
    w_r, w_h, wq, wkv = dims["w_rwkv"], dims["w_hyena"], dims["wq"], dims["wkv"]
    rwkv_cols = 3 * w_r + 2 * P["rwkv_w2"].shape[2] + 2 * P["rwkv_a2"].shape[2] + P["rwkv_g2"].shape[1]
    rwkv_pad = -(-rwkv_cols // (2 * LANES)) * (2 * LANES)
    hy_cols = 3 * w_h
    w_in = P["w_in"][l]
    lp = {"rwkv_w": w_r}
    lp["w_in_r"] = _pad_to(w_in[:, :rwkv_cols], (w_in.shape[0], rwkv_pad)).astype(BF16)
    lp["w_in_h"] = w_in[:, rwkv_cols:rwkv_cols + hy_cols].astype(BF16)
    lp["w_in_a"] = w_in[:, rwkv_cols + hy_cols:].astype(BF16)
    lp["rwkv_shift"] = _pad_to(P["rwkv_shift"][l], (3, rwkv_pad))
    lp["rwkv_w0"] = P["rwkv_w0"][l]
    lp["rwkv_a0"] = P["rwkv_a0"][l]
    for nm in ("rwkv_kk", "rwkv_ka", "rwkv_rk", "rwkv_gn_g", "rwkv_gn_b", "hy_bias",
               "ln1_g", "ln1_b", "ln2_g", "ln2_b"):
        lp[nm] = P[nm][l][None, :]
    lp["w2_hi"], lp["w2_lo"] = _split2_host(_block_diag2(P["rwkv_w2"][l, 0], P["rwkv_w2"][l, 1]))
    lp["a2_hi"], lp["a2_lo"] = _split2_host(_block_diag2(P["rwkv_a2"][l, 0], P["rwkv_a2"][l, 1]))
    lp["g2_hi"], lp["g2_lo"] = _split2_host(P["rwkv_g2"][l])
    lp["ones_head"] = _block_ones(w_r, HEAD)
    lp["ones_pair"] = _block_ones(LANES, HEAD)
    lp["hy_short"] = P["hy_short"][l]
    ffn_w = P["hy_w1"].shape[2]
    lp["hy_w1"] = _pad_to(P["hy_w1"][l], (LANES, LANES))
    lp["hy_b1"] = _pad_to(P["hy_b1"][l][None, :], (1, LANES))
    lp["hy_freq"] = _pad_to(P["hy_freq"][l], (2, LANES))
    lp["hy_w2"] = _pad_to(P["hy_w2"][l], (LANES, LANES))
    lp["hy_b2"] = _pad_to(P["hy_b2"][l][None, :], (1, LANES))
    lp["hy_w3"] = _pad_to(P["hy_w3"][l], (LANES, 2 * w_h))
    lp["hy_decay"] = P["hy_decay"][l].reshape(1, 2 * w_h)
    del ffn_w
    lp["attn_qn"] = jnp.tile(P["attn_qn"][l], LANES // HEAD)[None, :]
    lp["attn_kn"] = jnp.tile(P["attn_kn"][l], LANES // HEAD)[None, :]
    lp["w_out"] = P["w_out"][l].astype(BF16)
    lp["ffn_up"] = P["ffn_up"][l].astype(BF16)
    lp["ffn_conv"] = P["ffn_conv"][l]
    lp["ffn_down"] = P["ffn_down"][l].astype(BF16)
    return lp


def _trunk_layer(x, mod6, lp, dims, rope_tabs, ctx_kv, s0, dft, z_feat, tiles):
    b, l, d = x.shape
    sh1, sc1, g1, sh2, sc2, g2 = mod6
    shared = sh1.shape[0] == 1
    alpha = dims["alpha"]
    w_r, w_h, wq, wkv = dims["w_rwkv"], dims["w_hyena"], dims["wq"], dims["wkv"]
    nh = w_r // HEAD
    tl_mm, tl_ew = tiles["mm"], tiles["ew"]

    xm = x.reshape(1, b * l, d) if shared else x
    p_r = _mod_proj(xm, sc1, sh1, lp["w_in_r"], tl_mm).reshape(b, l, -1)
    p_h = _mod_proj(xm, sc1, sh1, lp["w_in_h"], tl_mm).reshape(b, l, -1)
    p_a = _mod_proj(xm, sc1, sh1, lp["w_in_a"], tl_mm).reshape(b, l, -1)

    (r, v, kk, wf, wb, kdf, kdb, bf, bb, bonus, gate) = _rwkv_prep(p_r, lp, tl_ew, l)
    flat = lambda a: a.reshape(b * nh, l * HEAD)
    yf, yb, s_fin = _wkv_scan(*(flat(a) for a in (r, v, kk, wf, wb, kdf, kdb, bf, bb)), s0)
    y_r = _rwkv_post(yf.reshape(b, nh, l, HEAD), yb.reshape(b, nh, l, HEAD), bonus, gate,
                     lp["rwkv_gn_g"], lp["rwkv_gn_b"], tl_ew)

    x0, uh, ul = _hy_prep(p_h, lp["hy_short"], tl_ew, l)
    hh, hl = _hy_mlp(z_feat, lp, min(l, 256))
    tf = min(l, 256)
    hre, him = _dft_filter(dft, hh, hl, tf)
    p4 = _dft_signal(dft, uh, ul, hre, him, tf)
    y_h = _dft_inverse(dft, p4, x0, uh, ul, lp["hy_bias"], tf)

    q, k, vv, k_tok = _attn_prep(p_a, lp, rope_tabs, tl_ew, wq, wkv)
    if ctx_kv is not None:
        ck, cv = ctx_kv
        k = jnp.concatenate([k, jnp.swapaxes(ck, 1, 2).astype(BF16)], axis=2)
        vv = jnp.concatenate([vv, jnp.swapaxes(cv, 1, 2).astype(BF16)], axis=2)
    y_a = _attention(q, k, vv, tiles["tq"])

    def m(a):
        return a.reshape(1, b * l, a.shape[-1]) if shared else a

    x1 = _out_proj(m(y_r), m(y_h), m(y_a), xm, g1, lp["w_out"], lp["ln1_g"], lp["ln1_b"],
                   tl_mm, alpha)
    x2 = _ffn(x1, sc2, sh2, g2, lp["ffn_up"], lp["ffn_conv"], lp["ffn_down"],
              lp["ln2_g"], lp["ln2_b"], tl_mm, tiles["ffn_tn"], l, alpha)
    v_tok = p_a[..., wq + wkv:]
    return x2.reshape(b, l, d), k_tok, v_tok, s_fin


def kernel(x_prompt, x_sample, cache_k, cache_v, state_rwkv, c, c_ctx, w_mod, b_mod, w_in, rwkv_shift, rwkv_w0, rwkv_w2, rwkv_a0, rwkv_a2, rwkv_kk, rwkv_ka, rwkv_rk, rwkv_g2, rwkv_gn_g, rwkv_gn_b, hy_short, hy_w1, hy_b1, hy_freq, hy_w2, hy_b2, hy_w3, hy_decay, hy_bias, attn_qn, attn_kn, w_out, ln1_g, ln1_b, ln2_g, ln2_b, ffn_up, ffn_conv, ffn_down):
    P = dict(w_in=w_in, rwkv_shift=rwkv_shift, rwkv_w0=rwkv_w0, rwkv_w2=rwkv_w2, rwkv_a0=rwkv_a0,
             rwkv_a2=rwkv_a2, rwkv_kk=rwkv_kk, rwkv_ka=rwkv_ka, rwkv_rk=rwkv_rk, rwkv_g2=rwkv_g2,
             rwkv_gn_g=rwkv_gn_g, rwkv_gn_b=rwkv_gn_b, hy_short=hy_short, hy_w1=hy_w1, hy_b1=hy_b1,
             hy_freq=hy_freq, hy_w2=hy_w2, hy_b2=hy_b2, hy_w3=hy_w3, hy_decay=hy_decay,
             hy_bias=hy_bias, attn_qn=attn_qn, attn_kn=attn_kn, w_out=w_out, ln1_g=ln1_g,
             ln1_b=ln1_b, ln2_g=ln2_g, ln2_b=ln2_b, ffn_up=ffn_up, ffn_conv=ffn_conv,
             ffn_down=ffn_down)
    depth, d = w_mod.shape[0], w_mod.shape[1]
    bc, lc, _ = x_prompt.shape
    bd, ld, _ = x_sample.shape
    nkv = cache_k.shape[3]
    nh = state_rwkv.shape[3]
    w_r = nh * HEAD
    w_h = hy_bias.shape[1]
    wkv = nkv * HEAD
    wq = w_in.shape[2] - (3 * w_r + 2 * rwkv_w2.shape[2] + 2 * rwkv_a2.shape[2] + rwkv_g2.shape[1]) \
        - 3 * w_h - 2 * wkv
    dims = dict(w_rwkv=w_r, w_hyena=w_h, wq=wq, wkv=wkv, alpha=(2 * depth) ** 0.25)

    rows = -(-(bd + 1) // SUBLANES) * SUBLANES
    cond = _pad_to(jnp.concatenate([c, c_ctx[None, :]], 0), (rows, d))
    mod = _modulation(cond, w_mod, b_mod)

    rope_tabs = _rope_tables(ld)
    dft_c, dft_d = _dft_mats(lc), _dft_mats(ld)
    n_bands = (hy_w1.shape[1] - 1) // 2
    z_c, z_d = _hyena_features(lc, n_bands), _hyena_features(ld, n_bands)
    tiles_c = dict(mm=min(512, bc * lc), ew=min(256, lc), tq=min(128, lc), ffn_tn=512)
    tiles_d = dict(mm=min(512, ld), ew=min(256, ld), tq=min(128, ld), ffn_tn=512)

    xp, xs = x_prompt, x_sample
    zero_state = jnp.zeros((2, bc * nh, HEAD * HEAD), F32)
    new_k, new_v, new_s = [], [], []
    for l in range(depth):
        lp = _layer_params(l, P, dims)
        mod_d = [m[:bd, None, :] for m in jnp.split(mod[l], 6, axis=-1)]
        mod_c = [m[bd:bd + 1, None, :] for m in jnp.split(mod[l], 6, axis=-1)]
        xp, k_c, v_c, s_c = _trunk_layer(xp, mod_c, lp, dims, None, None, zero_state, dft_c, z_c, tiles_c)
        new_k.append(k_c.reshape(bc, lc, nkv, HEAD))
        new_v.append(v_c.reshape(bc, lc, nkv, HEAD))
        new_s.append(jnp.swapaxes(s_c.reshape(2, bc, nh, HEAD, HEAD), 0, 1))
        s0 = jnp.swapaxes(state_rwkv[:, l], 0, 1).reshape(2, bd * nh, HEAD * HEAD)
        xs, _, _, _ = _trunk_layer(xs, mod_d, lp, dims, rope_tabs, (cache_k[:, l], cache_v[:, l]),
                                   s0, dft_d, z_d, tiles_d)
    return (xp, xs, jnp.stack(new_k, axis=1), jnp.stack(new_v, axis=1), jnp.stack(new_s, axis=1))
```

```python
import functools
import math

import jax
import jax.numpy as jnp
from jax import lax
from jax.experimental import pallas as pl
from jax.experimental.pallas import tpu as pltpu

F32 = jnp.float32
BF16 = jnp.bfloat16

HEAD = 64
LANES = 128
SUBLANES = 8
VMEM_LIMIT = 52 * 1024 * 1024
LN_EPS = 1e-5
QK_EPS = 1e-6
GN_EPS = 64e-5
ROPE_THETA = 10000.0
GRID_W = 64
SCAN_T = 16
WKV_TILES = 3
FFN_HALO = 16


def _cparams(*sem):
    return pltpu.CompilerParams(dimension_semantics=sem, vmem_limit_bytes=VMEM_LIMIT)


def _dot(a, b):
    return jnp.dot(a, b, preferred_element_type=F32)


def _hi_f32(x):
    u = pltpu.bitcast(x, jnp.uint32) & jnp.uint32(0xFFFF0000)
    return pltpu.bitcast(u, F32)


def _split2(x):
    h = _hi_f32(x)
    return h.astype(BF16), (x - h).astype(BF16)


def _split3(x):
    h1 = _hi_f32(x)
    r1 = x - h1
    h2 = _hi_f32(r1)
    return h1.astype(BF16), h2.astype(BF16), (r1 - h2).astype(BF16)


def _dot3(a, b_hi, b_lo):
    a_hi, a_lo = _split2(a)
    return _dot(a_hi, b_hi) + (_dot(a_lo, b_hi) + _dot(a_hi, b_lo))


def _segsum(x, ones_bd):
    h1, h2, h3 = _split3(x)
    return _dot(h1, ones_bd) + (_dot(h2, ones_bd) + _dot(h3, ones_bd))


def _layer_norm(x, g, b):
    mu = jnp.mean(x, axis=-1, keepdims=True)
    xc = x - mu
    var = jnp.mean(xc * xc, axis=-1, keepdims=True)
    return xc * lax.rsqrt(var + LN_EPS) * g + b


def _sigmoid(x):
    return 1.0 / (1.0 + jnp.exp(-x))


def _conv3(x, prev_row, next_row, w, row0, seq_len):
    tl = x.shape[0]
    row = lax.broadcasted_iota(jnp.int32, (tl, 1), 0)
    pos = (row0 + row) % seq_len
    xm = jnp.where(row == 0, prev_row, pltpu.roll(x, 1, 0))
    xp = jnp.where(row == tl - 1, next_row, pltpu.roll(x, tl - 1, 0))
    xm = jnp.where(pos == 0, 0.0, xm)
    xp = jnp.where(pos == seq_len - 1, 0.0, xp)
    return xm * w[0:1, :] + x * w[1:2, :] + xp * w[2:3, :]


def _halo_specs(tl, width, n_rows, halo=SUBLANES):
    r = tl // halo
    last = n_rows // halo - 1
    prev = pl.BlockSpec((1, halo, width), lambda b, i: (b, jnp.maximum(i * r - 1, 0), 0))
    nxt = pl.BlockSpec((1, halo, width), lambda b, i: (b, jnp.minimum((i + 1) * r, last), 0))
    return prev, nxt


def _bm(arr):
    if arr.shape[0] == 1:
        return lambda b: 0
    return lambda b: b


def _mod_kernel(c_ref, w_ref, b_ref, o_ref):
    c = c_ref[...]
    s = c * _sigmoid(c)
    w_hi, w_lo = _split2(w_ref[0])
    o_ref[0] = _dot3(s, w_hi, w_lo) + b_ref[0]


def _modulation(cond, w_mod, b_mod):
    depth, d, n = w_mod.shape
    rows = cond.shape[0]
    tn = 1024
    return pl.pallas_call(
        _mod_kernel,
        grid=(depth, n // tn),
        in_specs=[
            pl.BlockSpec((rows, d), lambda l, j: (0, 0)),
            pl.BlockSpec((1, d, tn), lambda l, j: (l, 0, j)),
            pl.BlockSpec((1, 1, tn), lambda l, j: (l, 0, j)),
        ],
        out_specs=pl.BlockSpec((1, rows, tn), lambda l, j: (l, 0, j)),
        out_shape=jax.ShapeDtypeStruct((depth, rows, n), F32),
        compiler_params=_cparams("parallel", "parallel"),
        name="modulation",
    )(cond, w_mod, b_mod.reshape(depth, 1, n))


def _proj_kernel(x_ref, sc_ref, sh_ref, w_ref, o_ref):
    h = (x_ref[0] * (1.0 + sc_ref[0]) + sh_ref[0]).astype(BF16)
    o_ref[0] = _dot(h, w_ref[...])


def _mod_proj(x, sc, sh, w, tl):
    bx, lx, d = x.shape
    n = w.shape[1]
    sel = _bm(sc)
    return pl.pallas_call(
        _proj_kernel,
        grid=(bx, lx // tl),
        in_specs=[
            pl.BlockSpec((1, tl, d), lambda b, i: (b, i, 0)),
            pl.BlockSpec((1, 1, d), lambda b, i: (sel(b), 0, 0)),
            pl.BlockSpec((1, 1, d), lambda b, i: (sel(b), 0, 0)),
            pl.BlockSpec((d, n), lambda b, i: (0, 0)),
        ],
        out_specs=pl.BlockSpec((1, tl, n), lambda b, i: (b, i, 0)),
        out_shape=jax.ShapeDtypeStruct((bx, lx, n), F32),
        compiler_params=_cparams("parallel", "parallel"),
        name="mod_proj",
    )(x, sc, sh, w)


def _store_packed(o_ref, pairs):
    for q, (xa, xb) in enumerate(pairs):
        for h in range(xa.shape[1] // HEAD):
            sl = slice(HEAD * h, HEAD * (h + 1))
            o_ref[0, h, q] = jnp.concatenate([xa[:, sl], xb[:, sl]], axis=-1)


def _rwkv_prep_kernel(p_ref, pp_ref, pn_ref, sw_ref, w0_ref, a0_ref, kkp_ref, ka_ref, rk_ref,
                      w2h_ref, w2l_ref, a2h_ref, a2l_ref, g2h_ref, g2l_ref, ones_ref,
                      pkf_o, pkb_o, bonus_o, g_o, *, seq_len, width):
    tl = p_ref.shape[1]
    i = pl.program_id(1)
    pc = _conv3(p_ref[0], pp_ref[0, SUBLANES - 1:SUBLANES, :], pn_ref[0, 0:1, :],
                sw_ref[...], i * tl, seq_len)
    w = width
    r = pc[:, 0:w]
    k = pc[:, w:2 * w]
    v = pc[:, 2 * w:3 * w]
    wd = pc[:, 3 * w:3 * w + LANES]
    ad = pc[:, 3 * w + LANES:3 * w + 2 * LANES]
    gd = pc[:, 3 * w + 2 * LANES:3 * w + 3 * LANES]
    ones_bd = ones_ref[...]

    g_o[0] = _dot3(_sigmoid(gd), g2h_ref[...], g2l_ref[...])
    kkr = k * kkp_ref[...]
    kk = kkr * lax.rsqrt(_segsum(kkr * kkr, ones_bd) + 1e-12)
    lw = _dot3(jnp.tanh(wd), w2h_ref[...], w2l_ref[...])
    la = _dot3(ad, a2h_ref[...], a2l_ref[...])

    rrk = r * rk_ref[...]
    bonus = jnp.zeros_like(r)
    for d, pk_o in enumerate((pkf_o, pkb_o)):
        z = w0_ref[d:d + 1, :] + lw[:, d * w:(d + 1) * w]
        w_log = -(jnp.maximum(-z, 0.0) + jnp.log(1.0 + jnp.exp(-jnp.abs(z)))) - 0.5
        decay = jnp.exp(-jnp.exp(w_log))
        a = _sigmoid(a0_ref[d:d + 1, :] + la[:, d * w:(d + 1) * w])
        kd = k * (1.0 + (a - 1.0) * ka_ref[...])
        _store_packed(pk_o, ((r, v), (kk, decay), (kd, kk * a)))
        bonus = bonus + _segsum(rrk * kd, ones_bd) * v
    bonus_o[0] = bonus


def _rwkv_prep(p_r, lp, tl, seq_len):
    b, l, wp = p_r.shape
    w = lp["rwkv_w"]
    nh = w // HEAD
    hm = jax.ShapeDtypeStruct((b, nh, WKV_TILES, l, LANES), F32)
    tok = jax.ShapeDtypeStruct((b, l, w), F32)
    prev, nxt = _halo_specs(tl, wp, l)

    def full(a):
        return pl.BlockSpec(a.shape, lambda b_, i: (0,) * a.ndim)

    consts = [lp["rwkv_shift"], lp["rwkv_w0"], lp["rwkv_a0"], lp["rwkv_kk"], lp["rwkv_ka"],
              lp["rwkv_rk"], lp["w2_hi"], lp["w2_lo"], lp["a2_hi"], lp["a2_lo"],
              lp["g2_hi"], lp["g2_lo"], lp["ones_head"]]
    hm_spec = pl.BlockSpec((1, nh, WKV_TILES, tl, LANES), lambda b_, i: (b_, 0, 0, i, 0))
    tok_spec = pl.BlockSpec((1, tl, w), lambda b_, i: (b_, i, 0))
    return pl.pallas_call(
        functools.partial(_rwkv_prep_kernel, seq_len=seq_len, width=w),
        grid=(b, l // tl),
        in_specs=[pl.BlockSpec((1, tl, wp), lambda b_, i: (b_, i, 0)), prev, nxt]
        + [full(a) for a in consts],
        out_specs=[hm_spec] * 2 + [tok_spec] * 2,
        out_shape=[hm] * 2 + [tok] * 2,
        compiler_params=_cparams("parallel", "parallel"),
        name="rwkv_prep",
    )(p_r, p_r, p_r, *consts)


def _allsum8(x):
    s = x[0:8]
    for q in range(1, x.shape[0] // SUBLANES):
        s = s + x[q * SUBLANES:(q + 1) * SUBLANES]
    s = s + pltpu.roll(s, 4, 0)
    s = s + pltpu.roll(s, 2, 0)
    return s + pltpu.roll(s, 1, 0)


def _wkv_kernel(pf_ref, pb_ref, s0_ref, yf_ref, yb_ref, so_ref, S, OPS, YS, ACC):
    i = pl.program_id(1)
    n = pl.num_programs(1)
    t_steps = pf_ref.shape[3]
    stride = WKV_TILES * t_steps
    pf2 = pf_ref.reshape(HEAD * stride, LANES)
    pb2 = pb_ref.reshape(HEAD * stride, LANES)
    yf2 = yf_ref.reshape(HEAD * t_steps, LANES)
    yb2 = yb_ref.reshape(HEAD * t_steps, LANES)

    @pl.when(i == 0)
    def _():
        S[...] = jnp.concatenate([s0_ref[0], s0_ref[1]], axis=0).T

    R0, V0, KK0, W0, KD0, B0, WR0 = (q * HEAD for q in range(7))

    for t in range(t_steps):
        tiles = []
        for q in range(WKV_TILES):
            f = pf2[pl.ds(q * t_steps + t, HEAD, stride=stride), :]
            b = pb2[pl.ds(q * t_steps + (t_steps - 1 - t), HEAD, stride=stride), :]
            tiles.append(jnp.concatenate([f, b], axis=0).T)
            OPS[t, q * LANES:(q + 1) * LANES, :] = tiles[q]
        OPS[t, WR0:WR0 + HEAD, :] = tiles[1][HEAD:] * tiles[0][:HEAD]

    def accumulate(j, state_tile, k, sa, y0):
        return (sa + state_tile * OPS[j, KK0 + k:KK0 + k + 1, :],
                y0 + state_tile * OPS[j, WR0 + k:WR0 + k + 1, :])

    sa = jnp.zeros((HEAD, LANES), F32)
    y0 = jnp.zeros((HEAD, LANES), F32)
    for k in range(HEAD):
        sa, y0 = accumulate(0, S[k * HEAD:(k + 1) * HEAD, :], k, sa, y0)
    ACC[0:HEAD, :] = sa
    ACC[HEAD:2 * HEAD, :] = y0

    def step(j, carry):
        jn = jnp.minimum(j + 1, t_steps - 1)
        sa = ACC[0:HEAD, :]
        y0 = ACC[HEAD:2 * HEAD, :]
        vv = OPS[j, V0:V0 + HEAD, :]
        r = OPS[j, R0:R0 + HEAD, :]
        br = jnp.tile(_allsum8(OPS[j, B0:B0 + HEAD, :] * r), (HEAD // SUBLANES, 1))
        kr = jnp.tile(_allsum8(OPS[j, KD0:KD0 + HEAD, :] * r), (HEAD // SUBLANES, 1))
        YS[j] = y0 - sa * br + vv * kr
        san = jnp.zeros((HEAD, LANES), F32)
        y0n = jnp.zeros((HEAD, LANES), F32)
        for k in range(HEAD):
            rows = slice(k * HEAD, (k + 1) * HEAD)
            snew = (S[rows, :] * OPS[j, W0 + k:W0 + k + 1, :]
                    - sa * OPS[j, B0 + k:B0 + k + 1, :]
                    + vv * OPS[j, KD0 + k:KD0 + k + 1, :])
            S[rows, :] = snew
            san, y0n = accumulate(jn, snew, k, san, y0n)
        ACC[0:HEAD, :] = san
        ACC[HEAD:2 * HEAD, :] = y0n
        return carry

    lax.fori_loop(0, t_steps, step, 0)

    zeros = jnp.zeros((HEAD, LANES), F32)
    for t in range(t_steps):
        yt = jnp.concatenate([YS[t], zeros], axis=0).T
        yf2[pl.ds(t, HEAD, stride=t_steps), :] = yt[:HEAD]
        yb2[pl.ds(t_steps - 1 - t, HEAD, stride=t_steps), :] = yt[HEAD:]

    @pl.when(i == n - 1)
    def _():
        st = S[...].T
        so_ref[0] = st[0:HEAD]
        so_ref[1] = st[HEAD:2 * HEAD]


def _wkv_scan(pkf, pkb, s0):
    b, nh, _, l, _ = pkf.shape
    gb = HEAD // nh
    assert gb * nh == HEAD and b % gb == 0 and l % SCAN_T == 0
    groups = b // gb
    n = l // SCAN_T
    pk_blk = (gb, nh, WKV_TILES, SCAN_T, LANES)
    y_blk = (gb, nh, SCAN_T, LANES)
    st = pl.BlockSpec((2, HEAD, HEAD * HEAD), lambda g, i: (0, g, 0))
    y_shape = jax.ShapeDtypeStruct((b, nh, l, LANES), F32)
    return pl.pallas_call(
        _wkv_kernel,
        grid=(groups, n),
        in_specs=[pl.BlockSpec(pk_blk, lambda g, i: (g, 0, 0, i, 0)),
                  pl.BlockSpec(pk_blk, lambda g, i: (g, 0, 0, n - 1 - i, 0)), st],
        out_specs=[pl.BlockSpec(y_blk, lambda g, i: (g, 0, i, 0)),
                   pl.BlockSpec(y_blk, lambda g, i: (g, 0, n - 1 - i, 0)), st],
        out_shape=[y_shape, y_shape, jax.ShapeDtypeStruct((2, b * nh, HEAD * HEAD), F32)],
        scratch_shapes=[pltpu.VMEM((HEAD * HEAD, LANES), F32),
                        pltpu.VMEM((SCAN_T, 7 * HEAD, LANES), F32),
                        pltpu.VMEM((SCAN_T, HEAD, LANES), F32),
                        pltpu.VMEM((2 * HEAD, LANES), F32)],
        compiler_params=_cparams("parallel", "arbitrary"),
        name="wkv_scan",
    )(pkf, pkb, s0)


def _rwkv_post_kernel(yf_ref, yb_ref, bonus_ref, g_ref, gng_ref, gnb_ref, o_ref):
    y = yf_ref[0, :, :, 0:HEAD] + yb_ref[0, :, :, 0:HEAD]
    mu = jnp.mean(y, axis=-1, keepdims=True)
    yc = y - mu
    var = jnp.mean(yc * yc, axis=-1, keepdims=True)
    yn = yc * lax.rsqrt(var + GN_EPS)
    yt = jnp.concatenate([yn[h] for h in range(yn.shape[0])], axis=-1)
    o_ref[0] = (yt * gng_ref[...] + gnb_ref[...] + bonus_ref[0]) * g_ref[0]


def _rwkv_post(yf, yb, bonus, g, gn_g, gn_b, tl):
    b, nh, l, _ = yf.shape
    w = nh * HEAD
    hm = pl.BlockSpec((1, nh, tl, LANES), lambda b_, i: (b_, 0, i, 0))
    tok = pl.BlockSpec((1, tl, w), lambda b_, i: (b_, i, 0))
    vec = pl.BlockSpec((1, w), lambda b_, i: (0, 0))
    return pl.pallas_call(
        _rwkv_post_kernel,
        grid=(b, l // tl),
        in_specs=[hm, hm, tok, tok, vec, vec],
        out_specs=tok,
        out_shape=jax.ShapeDtypeStruct((b, l, w), F32),
        compiler_params=_cparams("parallel", "parallel"),
        name="rwkv_post",
    )(yf, yb, bonus, g, gn_g, gn_b)


def _hy_prep_kernel(p_ref, pp_ref, pn_ref, sw_ref, x0_o, uh_o, ul_o, *, seq_len, width):
    tl = p_ref.shape[1]
    i = pl.program_id(1)
    pc = _conv3(p_ref[0], pp_ref[0, SUBLANES - 1:SUBLANES, :], pn_ref[0, 0:1, :],
                sw_ref[...], i * tl, seq_len)
    w = width
    x0_o[0] = pc[:, 0:w]
    u = pc[:, w:2 * w] * pc[:, 2 * w:3 * w]
    hi, lo = _split2(u)
    uh_o[0] = hi
    ul_o[0] = lo


def _hy_prep(p_h, short_w, tl, seq_len):
    b, l, w3 = p_h.shape
    w = w3 // 3
    prev, nxt = _halo_specs(tl, w3, l)
    tok = pl.BlockSpec((1, tl, w), lambda b_, i: (b_, i, 0))
    return pl.pallas_call(
        functools.partial(_hy_prep_kernel, seq_len=seq_len, width=w),
        grid=(b, l // tl),
        in_specs=[pl.BlockSpec((1, tl, w3), lambda b_, i: (b_, i, 0)), prev, nxt,
                  pl.BlockSpec((3, w3), lambda b_, i: (0, 0))],
        out_specs=[tok, tok, tok],
        out_shape=[jax.ShapeDtypeStruct((b, l, w), F32), jax.ShapeDtypeStruct((b, l, w), BF16),
                   jax.ShapeDtypeStruct((b, l, w), BF16)],
        compiler_params=_cparams("parallel", "parallel"),
        name="hy_prep",
    )(p_h, p_h, p_h, short_w)


def _hy_mlp_kernel(z_ref, w1_ref, b1_ref, f_ref, w2_ref, b2_ref, w3_ref, dec_ref, hh_o, hl_o):
    z = z_ref[...]
    t01 = z[:, 0:1]
    w1h, w1l = _split2(w1_ref[...])
    w2h, w2l = _split2(w2_ref[...])
    w3h, w3l = _split2(w3_ref[...])
    h = jnp.sin(f_ref[0:1, :] * (_dot3(z, w1h, w1l) + b1_ref[...]))
    h = jnp.sin(f_ref[1:2, :] * (_dot3(h, w2h, w2l) + b2_ref[...]))
    h = _dot3(h, w3h, w3l) * jnp.exp(-t01 * jnp.abs(dec_ref[...]))
    hi, lo = _split2(h)
    hh_o[0] = hi
    hl_o[0] = lo


def _hy_mlp(z, lp, tl):
    n = z.shape[0]
    c2 = lp["hy_w3"].shape[1]

    def full(a):
        return pl.BlockSpec(a.shape, lambda i: (0,) * a.ndim)

    consts = [lp["hy_w1"], lp["hy_b1"], lp["hy_freq"], lp["hy_w2"], lp["hy_b2"], lp["hy_w3"],
              lp["hy_decay"]]
    out = pl.BlockSpec((1, tl, c2), lambda i: (0, i, 0))
    return pl.pallas_call(
        _hy_mlp_kernel,
        grid=(n // tl,),
        in_specs=[pl.BlockSpec((tl, LANES), lambda i: (i, 0))] + [full(a) for a in consts],
        out_specs=[out, out],
        out_shape=[jax.ShapeDtypeStruct((1, n, c2), BF16)] * 2,
        compiler_params=_cparams("parallel"),
        name="hy_mlp",
    )(z, *consts)


def _dft3(fh_ref, fl_ref, xh, xl):
    fh = fh_ref[...]
    return _dot(fh, xh) + (_dot(fl_ref[...], xh) + _dot(fh, xl))


def _dft_filter_kernel(frh, frl, fih, fil, xh_ref, xl_ref, hre_o, him_o):
    xh, xl = xh_ref[0], xl_ref[0]
    w = xh.shape[1] // 2
    xre = _dft3(frh, frl, xh, xl)
    xim = _dft3(fih, fil, xh, xl)
    hre_o[...] = xre[:, :w] + xre[:, w:]
    row = lax.broadcasted_iota(jnp.int32, (xre.shape[0], 1), 0) + pl.program_id(0) * xre.shape[0]
    him_o[...] = jnp.where(row == 0, xim[:, :w] + xim[:, w:], xim[:, :w] - xim[:, w:])


def _dft_filter(mats, hh, hl, tf):
    n = hh.shape[1]
    w = hh.shape[2] // 2
    ft = pl.BlockSpec((tf, n), lambda i: (i, 0))
    xs = pl.BlockSpec((1, n, 2 * w), lambda i: (0, 0, 0))
    out = pl.BlockSpec((tf, w), lambda i: (i, 0))
    return pl.pallas_call(
        _dft_filter_kernel,
        grid=(n // tf,),
        in_specs=[ft, ft, ft, ft, xs, xs],
        out_specs=[out, out],
        out_shape=[jax.ShapeDtypeStruct((n, w), F32)] * 2,
        compiler_params=_cparams("parallel"),
        name="dft_filter",
    )(mats["fre_hi"], mats["fre_lo"], mats["fim_hi"], mats["fim_lo"], hh, hl)


def _dft_signal_kernel(frh, frl, fih, fil, xh_ref, xl_ref, hre_ref, him_ref,
                       prh_o, prl_o, pih_o, pil_o):
    xh, xl = xh_ref[0], xl_ref[0]
    xre = _dft3(frh, frl, xh, xl)
    xim = _dft3(fih, fil, xh, xl)
    hre, him = hre_ref[...], him_ref[...]
    row = lax.broadcasted_iota(jnp.int32, (xre.shape[0], 1), 0) + pl.program_id(1) * xre.shape[0]
    pre = jnp.where(row == 0, xre * hre, xre * hre - xim * him)
    pim = jnp.where(row == 0, xim * him, xre * him + xim * hre)
    prh_o[0], prl_o[0] = _split2(pre)
    pih_o[0], pil_o[0] = _split2(pim)


def _dft_signal(mats, uh, ul, hre, him, tf):
    b, n, w = uh.shape
    ft = pl.BlockSpec((tf, n), lambda b_, i: (i, 0))
    xs = pl.BlockSpec((1, n, w), lambda b_, i: (b_, 0, 0))
    hs = pl.BlockSpec((tf, w), lambda b_, i: (i, 0))
    out = pl.BlockSpec((1, tf, w), lambda b_, i: (b_, i, 0))
    return pl.pallas_call(
        _dft_signal_kernel,
        grid=(b, n // tf),
        in_specs=[ft, ft, ft, ft, xs, xs, hs, hs],
        out_specs=[out] * 4,
        out_shape=[jax.ShapeDtypeStruct((b, n, w), BF16)] * 4,
        compiler_params=_cparams("parallel", "parallel"),
        name="dft_signal",
    )(mats["fre_hi"], mats["fre_lo"], mats["fim_hi"], mats["fim_lo"], uh, ul, hre, him)


def _dft_inverse_kernel(grh, grl, gih, gil, prh, prl, pih, pil, x0_ref, uh_ref, ul_ref, bias_ref,
                        o_ref):
    y = _dft3(grh, grl, prh[0], prl[0]) + _dft3(gih, gil, pih[0], pil[0])
    u = uh_ref[0].astype(F32) + ul_ref[0].astype(F32)
    o_ref[0] = x0_ref[0] * (y + u * bias_ref[...])


def _dft_inverse(mats, p4, x0, uh, ul, bias, tt):
    b, n, w = uh.shape
    gt = pl.BlockSpec((tt, n), lambda b_, i: (i, 0))
    ps = pl.BlockSpec((1, n, w), lambda b_, i: (b_, 0, 0))
    tok = pl.BlockSpec((1, tt, w), lambda b_, i: (b_, i, 0))
    return pl.pallas_call(
        _dft_inverse_kernel,
        grid=(b, n // tt),
        in_specs=[gt, gt, gt, gt, ps, ps, ps, ps, tok, tok, tok,
                  pl.BlockSpec((1, w), lambda b_, i: (0, 0))],
        out_specs=tok,
        out_shape=jax.ShapeDtypeStruct((b, n, w), F32),
        compiler_params=_cparams("parallel", "parallel"),
        name="dft_inverse",
    )(mats["gre_hi"], mats["gre_lo"], mats["gim_hi"], mats["gim_lo"], *p4, x0, uh, ul, bias)


def _swap_halves(x):
    lane = lax.broadcasted_iota(jnp.int32, x.shape, 1)
    half = HEAD // 2
    return jnp.where(lane % HEAD < half, pltpu.roll(x, LANES - half, 1), pltpu.roll(x, half, 1))


def _attn_prep_kernel(*refs, rope, wq, wkv):
    if rope:
        p_ref, qn_ref, kn_ref, ones_ref, cos_ref, sin_ref, q_o, k_o, v_o, kt_o = refs
    else:
        p_ref, qn_ref, kn_ref, ones_ref, q_o, k_o, v_o, kt_o = refs
    ones_bd = ones_ref[...]

    def norm_rope(x, gain):
        ms = _segsum(x * x, ones_bd) * (1.0 / HEAD)
        xn = x * lax.rsqrt(ms + QK_EPS) * gain
        if rope:
            xn = xn * cos_ref[...] + _swap_halves(xn) * sin_ref[...]
        return xn

    scale = HEAD ** -0.5
    for c in range(wq // LANES):
        xq = norm_rope(p_ref[0, :, c * LANES:(c + 1) * LANES], qn_ref[...]) * scale
        q_o[0, 2 * c] = xq[:, :HEAD].astype(BF16)
        q_o[0, 2 * c + 1] = xq[:, HEAD:].astype(BF16)
    for c in range(wkv // LANES):
        x = p_ref[0, :, wq + c * LANES:wq + (c + 1) * LANES]
        ms = _segsum(x * x, ones_bd) * (1.0 / HEAD)
        xk = x * lax.rsqrt(ms + QK_EPS) * kn_ref[...]
        kt_o[0, :, c * LANES:(c + 1) * LANES] = xk
        if rope:
            xk = xk * cos_ref[...] + _swap_halves(xk) * sin_ref[...]
        k_o[0, 2 * c] = xk[:, :HEAD].astype(BF16)
        k_o[0, 2 * c + 1] = xk[:, HEAD:].astype(BF16)
        xv = p_ref[0, :, wq + wkv + c * LANES:wq + wkv + (c + 1) * LANES]
        v_o[0, 2 * c] = xv[:, :HEAD].astype(BF16)
        v_o[0, 2 * c + 1] = xv[:, HEAD:].astype(BF16)


def _attn_prep(p_a, lp, rope_tabs, tl, wq, wkv):
    b, l, wa = p_a.shape
    nq, nkv = wq // HEAD, wkv // HEAD
    rope = rope_tabs is not None
    vec = pl.BlockSpec((1, LANES), lambda b_, i: (0, 0))
    in_specs = [pl.BlockSpec((1, tl, wa), lambda b_, i: (b_, i, 0)), vec, vec,
                pl.BlockSpec((LANES, LANES), lambda b_, i: (0, 0))]
    args = [p_a, lp["attn_qn"], lp["attn_kn"], lp["ones_pair"]]
    if rope:
        tab = pl.BlockSpec((tl, LANES), lambda b_, i: (i, 0))
        in_specs += [tab, tab]
        args += list(rope_tabs)
    return pl.pallas_call(
        functools.partial(_attn_prep_kernel, rope=rope, wq=wq, wkv=wkv),
        grid=(b, l // tl),
        in_specs=in_specs,
        out_specs=[pl.BlockSpec((1, nq, tl, HEAD), lambda b_, i: (b_, 0, i, 0)),
                   pl.BlockSpec((1, nkv, tl, HEAD), lambda b_, i: (b_, 0, i, 0)),
                   pl.BlockSpec((1, nkv, tl, HEAD), lambda b_, i: (b_, 0, i, 0)),
                   pl.BlockSpec((1, tl, wkv), lambda b_, i: (b_, i, 0))],
        out_shape=[jax.ShapeDtypeStruct((b, nq, l, HEAD), BF16),
                   jax.ShapeDtypeStruct((b, nkv, l, HEAD), BF16),
                   jax.ShapeDtypeStruct((b, nkv, l, HEAD), BF16),
                   jax.ShapeDtypeStruct((b, l, wkv), F32)],
        compiler_params=_cparams("parallel", "parallel"),
        name="attn_prep",
    )(*args)


def _attn_kernel(q_ref, k_ref, v_ref, o_ref):
    k, v = k_ref[0, 0], v_ref[0, 0]
    outs = []
    for h in range(q_ref.shape[1]):
        s = lax.dot_general(q_ref[0, h], k, (((1,), (1,)), ((), ())), preferred_element_type=F32)
        m = jnp.max(s, axis=-1, keepdims=True)
        p = jnp.exp(s - m)
        den = jnp.sum(p, axis=-1, keepdims=True)
        outs.append(_dot(p.astype(BF16), v) / den)
    o_ref[0] = jnp.concatenate(outs, axis=-1)


def _attention(q, k, v, tq):
    b, nq, l, _ = q.shape
    nkv, lk = k.shape[1], k.shape[2]
    g = nq // nkv
    kv = pl.BlockSpec((1, 1, lk, HEAD), lambda b_, h, i: (b_, h, 0, 0))
    return pl.pallas_call(
        _attn_kernel,
        grid=(b, nkv, l // tq),
        in_specs=[pl.BlockSpec((1, g, tq, HEAD), lambda b_, h, i: (b_, h, i, 0)), kv, kv],
        out_specs=pl.BlockSpec((1, tq, g * HEAD), lambda b_, h, i: (b_, i, h)),
        out_shape=jax.ShapeDtypeStruct((b, l, nq * HEAD), F32),
        compiler_params=_cparams("parallel", "parallel", "parallel"),
        name="attention",
    )(q, k, v)


def _out_proj_kernel(yr_ref, yh_ref, ya_ref, x_ref, g_ref, w_ref, lng_ref, lnb_ref, o_ref, *, alpha):
    wr, wh = yr_ref.shape[2], yh_ref.shape[2]
    mix = _dot(yr_ref[0].astype(BF16), w_ref[0:wr, :])
    mix = mix + _dot(yh_ref[0].astype(BF16), w_ref[wr:wr + wh, :])
    mix = mix + _dot(ya_ref[0].astype(BF16), w_ref[wr + wh:, :])
    o_ref[0] = _layer_norm(alpha * x_ref[0] + g_ref[0] * mix, lng_ref[...], lnb_ref[...])


def _out_proj(y_r, y_h, y_a, x, gate, w_out, ln_g, ln_b, tl, alpha):
    bx, lx, d = x.shape
    sel = _bm(gate)

    def tok(a):
        return pl.BlockSpec((1, tl, a.shape[2]), lambda b, i: (b, i, 0))

    vec = pl.BlockSpec((1, d), lambda b, i: (0, 0))
    return pl.pallas_call(
        functools.partial(_out_proj_kernel, alpha=alpha),
        grid=(bx, lx // tl),
        in_specs=[tok(y_r), tok(y_h), tok(y_a), tok(x),
                  pl.BlockSpec((1, 1, d), lambda b, i: (sel(b), 0, 0)),
                  pl.BlockSpec(w_out.shape, lambda b, i: (0, 0)), vec, vec],
        out_specs=tok(x),
        out_shape=jax.ShapeDtypeStruct((bx, lx, d), F32),
        compiler_params=_cparams("parallel", "parallel"),
        name="out_proj",
    )(y_r, y_h, y_a, x, gate, w_out, ln_g, ln_b)


def _ffn_kernel(x_ref, xp_ref, xn_ref, sc_ref, sh_ref, g_ref, wa_ref, wb_ref, ca_ref, cb_ref,
                wd_ref, lng_ref, lnb_ref, o_ref, h_s, acc_s, *, seq_len, alpha):
    tl = x_ref.shape[1]
    i = pl.program_id(1)
    j = pl.program_id(2)
    halo = FFN_HALO

    @pl.when(j == 0)
    def _():
        sc, sh = 1.0 + sc_ref[0], sh_ref[0]
        h_s[0:halo] = (xp_ref[0] * sc + sh).astype(BF16)
        h_s[halo:halo + tl] = (x_ref[0] * sc + sh).astype(BF16)
        h_s[halo + tl:] = (xn_ref[0] * sc + sh).astype(BF16)
        acc_s[...] = jnp.zeros_like(acc_s)

    row = lax.broadcasted_iota(jnp.int32, (tl, 1), 0)
    pos = (i * tl + row) % seq_len
    first = pos == 0
    last = pos == seq_len - 1
    h = h_s[...]

    def conv_up(w_ref, c_ref):
        u = _dot(h, w_ref[...])
        um = jnp.where(first, 0.0, pltpu.roll(u, 1, 0)[halo:halo + tl])
        up = jnp.where(last, 0.0, pltpu.roll(u, tl + 2 * halo - 1, 0)[halo:halo + tl])
        return um * c_ref[0:1, :] + u[halo:halo + tl] * c_ref[1:2, :] + up * c_ref[2:3, :]

    a = conv_up(wa_ref, ca_ref)
    b = conv_up(wb_ref, cb_ref)
    f = (a * _sigmoid(a) * b).astype(BF16)
    acc_s[...] += _dot(f, wd_ref[...])

    @pl.when(j == pl.num_programs(2) - 1)
    def _():
        o_ref[0] = _layer_norm(alpha * x_ref[0] + g_ref[0] * acc_s[...], lng_ref[...], lnb_ref[...])


def _ffn(x, sc, sh, gate, w_up, conv_w, w_down, ln_g, ln_b, tl, tn, seq_len, alpha):
    bx, lx, d = x.shape
    dff = w_down.shape[0]
    nj = dff // tn
    sel = _bm(sc)
    prev, nxt = _halo_specs(tl, d, lx, FFN_HALO)
    prev3 = pl.BlockSpec(prev.block_shape, lambda b, i, j: prev.index_map(b, i))
    nxt3 = pl.BlockSpec(nxt.block_shape, lambda b, i, j: nxt.index_map(b, i))
    mod = pl.BlockSpec((1, 1, d), lambda b, i, j: (sel(b), 0, 0))
    vec = pl.BlockSpec((1, d), lambda b, i, j: (0, 0))
    tok = pl.BlockSpec((1, tl, d), lambda b, i, j: (b, i, 0))
    return pl.pallas_call(
        functools.partial(_ffn_kernel, seq_len=seq_len, alpha=alpha),
        grid=(bx, lx // tl, nj),
        in_specs=[tok, prev3, nxt3, mod, mod, mod,
                  pl.BlockSpec((d, tn), lambda b, i, j: (0, j)),
                  pl.BlockSpec((d, tn), lambda b, i, j: (0, j + nj)),
                  pl.BlockSpec((3, tn), lambda b, i, j: (0, j)),
                  pl.BlockSpec((3, tn), lambda b, i, j: (0, j + nj)),
                  pl.BlockSpec((tn, d), lambda b, i, j: (j, 0)), vec, vec],
        out_specs=tok,
        out_shape=jax.ShapeDtypeStruct((bx, lx, d), F32),
        scratch_shapes=[pltpu.VMEM((tl + 2 * FFN_HALO, d), BF16), pltpu.VMEM((tl, d), F32)],
        compiler_params=_cparams("parallel", "parallel", "arbitrary"),
        name="conv_ffn",
    )(x, x, x, sc, sh, gate, w_up, w_up, conv_w, conv_w, w_down, ln_g, ln_b)


def _split2_host(x):
    hi = lax.bitcast_convert_type(lax.bitcast_convert_type(x, jnp.uint32) & jnp.uint32(0xFFFF0000), F32)
    return hi.astype(BF16), (x - hi).astype(BF16)


def _dft_mats(n):
    big = 2 * n
    k = jnp.arange(n, dtype=jnp.int32)[:, None]
    t = jnp.arange(n, dtype=jnp.int32)[None, :]
    ang = ((k * t) % big).astype(F32) * (2.0 * math.pi / big)
    fre = jnp.cos(ang)
    fim = jnp.where(k == 0, jnp.where(t % 2 == 0, 1.0, -1.0), -jnp.sin(ang))
    scale = jnp.where(k == 0, 1.0 / big, 2.0 / big)
    out = {}
    for name, m in (("fre", fre), ("fim", fim), ("gre", (fre * scale).T), ("gim", (fim * scale).T)):
        out[name + "_hi"], out[name + "_lo"] = _split2_host(m)
    return out


def _hyena_features(n, n_bands):
    t01 = jnp.linspace(0.0, 1.0, n, dtype=F32)[:, None]
    pos = jnp.arange(n, dtype=F32)[:, None]
    bands = jnp.linspace(1e-4, n_bands - 1, n_bands, dtype=F32)[None, :]
    ang = (2.0 * math.pi / n) * pos * bands
    z = jnp.concatenate([t01, jnp.cos(ang), -jnp.sin(ang)], -1)
    return jnp.pad(z, ((0, 0), (0, LANES - z.shape[1])))


def _rope_tables(n_tokens):
    rows = n_tokens // GRID_W
    row = jnp.repeat(jnp.arange(rows, dtype=F32), GRID_W)
    col = jnp.tile(jnp.arange(GRID_W, dtype=F32), rows)
    n_freq = HEAD // 4
    inv = ROPE_THETA ** (-jnp.arange(n_freq, dtype=F32) / n_freq)
    ang = jnp.concatenate([row[:, None] * inv, col[:, None] * inv], -1)
    cos, sin = jnp.cos(ang), jnp.sin(ang)
    cos2 = jnp.tile(jnp.concatenate([cos, cos], -1), (1, LANES // HEAD))
    sin2 = jnp.tile(jnp.concatenate([-sin, sin], -1), (1, LANES // HEAD))
    return cos2, sin2


def _block_ones(n, group):
    idx = jnp.arange(n) // group
    return (idx[:, None] == idx[None, :]).astype(BF16)


def _pad_to(a, shape):
    return jnp.pad(a, [(0, s - d) for d, s in zip(a.shape, shape)])


def _block_diag2(a, b):
    za = jnp.zeros((a.shape[0], b.shape[1]), a.dtype)
    zb = jnp.zeros((b.shape[0], a.shape[1]), a.dtype)
    return jnp.concatenate([jnp.concatenate([a, za], 1), jnp.concatenate([zb, b], 1)], 0)


def _layer_params(l, P, dims):
    w_r, w_h, wq, wkv = dims["w_rwkv"], dims["w_hyena"], dims["wq"], dims["wkv"]
    rwkv_cols = 3 * w_r + 2 * P["rwkv_w2"].shape[2] + 2 * P["rwkv_a2"].shape[2] + P["rwkv_g2"].shape[1]
    rwkv_pad = -(-rwkv_cols // (2 * LANES)) * (2 * LANES)
    hy_cols = 3 * w_h
    w_in = P["w_in"][l]
    lp = {"rwkv_w": w_r}
    lp["w_in_r"] = _pad_to(w_in[:, :rwkv_cols], (w_in.shape[0], rwkv_pad)).astype(BF16)
    lp["w_in_h"] = w_in[:, rwkv_cols:rwkv_cols + hy_cols].astype(BF16)
    lp["w_in_a"] = w_in[:, rwkv_cols + hy_cols:].astype(BF16)
    lp["rwkv_shift"] = _pad_to(P["rwkv_shift"][l], (3, rwkv_pad))
    lp["rwkv_w0"] = P["rwkv_w0"][l]
    lp["rwkv_a0"] = P["rwkv_a0"][l]
    for nm in ("rwkv_kk", "rwkv_ka", "rwkv_rk", "rwkv_gn_g", "rwkv_gn_b", "hy_bias",
               "ln1_g", "ln1_b", "ln2_g", "ln2_b"):
        lp[nm] = P[nm][l][None, :]
    lp["w2_hi"], lp["w2_lo"] = _split2_host(_block_diag2(P["rwkv_w2"][l, 0], P["rwkv_w2"][l, 1]))
    lp["a2_hi"], lp["a2_lo"] = _split2_host(_block_diag2(P["rwkv_a2"][l, 0], P["rwkv_a2"][l, 1]))
    lp["g2_hi"], lp["g2_lo"] = _split2_host(P["rwkv_g2"][l])
    lp["ones_head"] = _block_ones(w_r, HEAD)
    lp["ones_pair"] = _block_ones(LANES, HEAD)
    lp["hy_short"] = P["hy_short"][l]
    ffn_w = P["hy_w1"].shape[2]
    lp["hy_w1"] = _pad_to(P["hy_w1"][l], (LANES, LANES))
    lp["hy_b1"] = _pad_to(P["hy_b1"][l][None, :], (1, LANES))
    lp["hy_freq"] = _pad_to(P["hy_freq"][l], (2, LANES))
    lp["hy_w2"] = _pad_to(P["hy_w2"][l], (LANES, LANES))
    lp["hy_b2"] = _pad_to(P["hy_b2"][l][None, :], (1, LANES))
    lp["hy_w3"] = _pad_to(P["hy_w3"][l], (LANES, 2 * w_h))
    lp["hy_decay"] = P["hy_decay"][l].reshape(1, 2 * w_h)
    del ffn_w
    lp["attn_qn"] = jnp.tile(P["attn_qn"][l], LANES // HEAD)[None, :]
    lp["attn_kn"] = jnp.tile(P["attn_kn"][l], LANES // HEAD)[None, :]
    lp["w_out"] = P["w_out"][l].astype(BF16)
    lp["ffn_up"] = P["ffn_up"][l].astype(BF16)
    lp["ffn_conv"] = P["ffn_conv"][l]
    lp["ffn_down"] = P["ffn_down"][l].astype(BF16)
    return lp


def _trunk_layer(x, mod6, lp, dims, rope_tabs, ctx_kv, s0, dft, z_feat, tiles):
    b, l, d = x.shape
    sh1, sc1, g1, sh2, sc2, g2 = mod6
    shared = sh1.shape[0] == 1
    alpha = dims["alpha"]
    w_r, w_h, wq, wkv = dims["w_rwkv"], dims["w_hyena"], dims["wq"], dims["wkv"]
    nh = w_r // HEAD
    tl_mm, tl_ew = tiles["mm"], tiles["ew"]

    xm = x.reshape(1, b * l, d) if shared else x
    p_r = _mod_proj(xm, sc1, sh1, lp["w_in_r"], tl_mm).reshape(b, l, -1)
    p_h = _mod_proj(xm, sc1, sh1, lp["w_in_h"], tl_mm).reshape(b, l, -1)
    p_a = _mod_proj(xm, sc1, sh1, lp["w_in_a"], tl_mm).reshape(b, l, -1)

    pkf, pkb, bonus, gate = _rwkv_prep(p_r, lp, tl_ew, l)
    yf, yb, s_fin = _wkv_scan(pkf, pkb, s0)
    y_r = _rwkv_post(yf, yb, bonus, gate, lp["rwkv_gn_g"], lp["rwkv_gn_b"], tl_ew)

    x0, uh, ul = _hy_prep(p_h, lp["hy_short"], tl_ew, l)
    hh, hl = _hy_mlp(z_feat, lp, min(l, 256))
    tf = min(l, 256)
    hre, him = _dft_filter(dft, hh, hl, tf)
    p4 = _dft_signal(dft, uh, ul, hre, him, tf)
    y_h = _dft_inverse(dft, p4, x0, uh, ul, lp["hy_bias"], tf)

    q, k, vv, k_tok = _attn_prep(p_a, lp, rope_tabs, tl_ew, wq, wkv)
    if ctx_kv is not None:
        ck, cv = ctx_kv
        k = jnp.concatenate([k, jnp.swapaxes(ck, 1, 2).astype(BF16)], axis=2)
        vv = jnp.concatenate([vv, jnp.swapaxes(cv, 1, 2).astype(BF16)], axis=2)
    y_a = _attention(q, k, vv, tiles["tq"])

    def m(a):
        return a.reshape(1, b * l, a.shape[-1]) if shared else a

    x1 = _out_proj(m(y_r), m(y_h), m(y_a), xm, g1, lp["w_out"], lp["ln1_g"], lp["ln1_b"],
                   tl_mm, alpha)
    x2 = _ffn(x1, sc2, sh2, g2, lp["ffn_up"], lp["ffn_conv"], lp["ffn_down"],
              lp["ln2_g"], lp["ln2_b"], tl_mm, tiles["ffn_tn"], l, alpha)
    v_tok = p_a[..., wq + wkv:]
    return x2.reshape(b, l, d), k_tok, v_tok, s_fin


def kernel(x_prompt, x_sample, cache_k, cache_v, state_rwkv, c, c_ctx, w_mod, b_mod, w_in, rwkv_shift, rwkv_w0, rwkv_w2, rwkv_a0, rwkv_a2, rwkv_kk, rwkv_ka, rwkv_rk, rwkv_g2, rwkv_gn_g, rwkv_gn_b, hy_short, hy_w1, hy_b1, hy_freq, hy_w2, hy_b2, hy_w3, hy_decay, hy_bias, attn_qn, attn_kn, w_out, ln1_g, ln1_b, ln2_g, ln2_b, ffn_up, ffn_conv, ffn_down):
    P = dict(w_in=w_in, rwkv_shift=rwkv_shift, rwkv_w0=rwkv_w0, rwkv_w2=rwkv_w2, rwkv_a0=rwkv_a0,
             rwkv_a2=rwkv_a2, rwkv_kk=rwkv_kk, rwkv_ka=rwkv_ka, rwkv_rk=rwkv_rk, rwkv_g2=rwkv_g2,
             rwkv_gn_g=rwkv_gn_g, rwkv_gn_b=rwkv_gn_b, hy_short=hy_short, hy_w1=hy_w1, hy_b1=hy_b1,
             hy_freq=hy_freq, hy_w2=hy_w2, hy_b2=hy_b2, hy_w3=hy_w3, hy_decay=hy_decay,
             hy_bias=hy_bias, attn_qn=attn_qn, attn_kn=attn_kn, w_out=w_out, ln1_g=ln1_g,
             ln1_b=ln1_b, ln2_g=ln2_g, ln2_b=ln2_b, ffn_up=ffn_up, ffn_conv=ffn_conv,
             ffn_down=ffn_down)
    depth, d = w_mod.shape[0], w_mod.shape[1]
    bc, lc, _ = x_prompt.shape
    bd, ld, _ = x_sample.shape
    nkv = cache_k.shape[3]
    nh = state_rwkv.shape[3]
    w_r = nh * HEAD
    w_h = hy_bias.shape[1]
    wkv = nkv * HEAD
    wq = w_in.shape[2] - (3 * w_r + 2 * rwkv_w2.shape[2] + 2 * rwkv_a2.shape[2] + rwkv_g2.shape[1]) \
        - 3 * w_h - 2 * wkv
    dims = dict(w_rwkv=w_r, w_hyena=w_h, wq=wq, wkv=wkv, alpha=(2 * depth) ** 0.25)

    rows = -(-(bd + 1) // SUBLANES) * SUBLANES
    cond = _pad_to(jnp.concatenate([c, c_ctx[None, :]], 0), (rows, d))
    mod = _modulation(cond, w_mod, b_mod)

    rope_tabs = _rope_tables(ld)
    dft_c, dft_d = _dft_mats(lc), _dft_mats(ld)
    n_bands = (hy_w1.shape[1] - 1) // 2
    z_c, z_d = _hyena_features(lc, n_bands), _hyena_features(ld, n_bands)
    tiles_c = dict(mm=min(512, bc * lc), ew=min(256, lc), tq=min(256, lc), ffn_tn=512)
    tiles_d = dict(mm=min(512, ld), ew=min(256, ld), tq=min(256, ld), ffn_tn=512)

    xp, xs = x_prompt, x_sample
    zero_state = jnp.zeros((2, bc * nh, HEAD * HEAD), F32)
    new_k, new_v, new_s = [], [], []
    for l in range(depth):
        lp = _layer_params(l, P, dims)
        mod_d = [m[:bd, None, :] for m in jnp.split(mod[l], 6, axis=-1)]
        mod_c = [m[bd:bd + 1, None, :] for m in jnp.split(mod[l], 6, axis=-1)]
        xp, k_c, v_c, s_c = _trunk_layer(xp, mod_c, lp, dims, None, None, zero_state, dft_c, z_c, tiles_c)
        new_k.append(k_c.reshape(bc, lc, nkv, HEAD))
        new_v.append(v_c.reshape(bc, lc, nkv, HEAD))
        new_s.append(jnp.transpose(s_c.reshape(2, bc, nh, HEAD, HEAD), (1, 0, 2, 4, 3)))
        s0 = jnp.transpose(state_rwkv[:, l], (1, 0, 2, 4, 3)).reshape(2, bd * nh, HEAD * HEAD)
        xs, _, _, _ = _trunk_layer(xs, mod_d, lp, dims, rope_tabs, (cache_k[:, l], cache_v[:, l]),
                                   s0, dft_d, z_d, tiles_d)
    return (xp, xs, jnp.stack(new_k, axis=1), jnp.stack(new_v, axis=1), jnp.stack(new_s, axis=1))
```

```python
import functools
import math

import jax
import jax.numpy as jnp
from jax import lax
from jax.experimental import pallas as pl
from jax.experimental.pallas import tpu as pltpu

F32 = jnp.float32
BF16 = jnp.bfloat16

HEAD = 64
LANES = 128
SUBLANES = 8
VMEM_LIMIT = 52 * 1024 * 1024
LN_EPS = 1e-5
QK_EPS = 1e-6
GN_EPS = 64e-5
ROPE_THETA = 10000.0
GRID_W = 64
SCAN_T = 16
WKV_TILES = 3
FFN_HALO = 16


def _cparams(*sem):
    return pltpu.CompilerParams(dimension_semantics=sem, vmem_limit_bytes=VMEM_LIMIT)


def _dot(a, b):
    return jnp.dot(a, b, preferred_element_type=F32)


def _hi_f32(x):
    u = pltpu.bitcast(x, jnp.uint32) & jnp.uint32(0xFFFF0000)
    return pltpu.bitcast(u, F32)


def _split2(x):
    h = _hi_f32(x)
    return h.astype(BF16), (x - h).astype(BF16)


def _split3(x):
    h1 = _hi_f32(x)
    r1 = x - h1
    h2 = _hi_f32(r1)
    return h1.astype(BF16), h2.astype(BF16), (r1 - h2).astype(BF16)


def _dot3(a, b_hi, b_lo):
    a_hi, a_lo = _split2(a)
    return _dot(a_hi, b_hi) + (_dot(a_lo, b_hi) + _dot(a_hi, b_lo))


def _segsum(x, ones_bd):
    h1, h2, h3 = _split3(x)
    return _dot(h1, ones_bd) + (_dot(h2, ones_bd) + _dot(h3, ones_bd))


def _layer_norm(x, g, b):
    mu = jnp.mean(x, axis=-1, keepdims=True)
    xc = x - mu
    var = jnp.mean(xc * xc, axis=-1, keepdims=True)
    return xc * lax.rsqrt(var + LN_EPS) * g + b


def _sigmoid(x):
    return 1.0 / (1.0 + jnp.exp(-x))


def _conv3(x, prev_row, next_row, w, row0, seq_len):
    tl = x.shape[0]
    row = lax.broadcasted_iota(jnp.int32, (tl, 1), 0)
    pos = (row0 + row) % seq_len
    xm = jnp.where(row == 0, prev_row, pltpu.roll(x, 1, 0))
    xp = jnp.where(row == tl - 1, next_row, pltpu.roll(x, tl - 1, 0))
    xm = jnp.where(pos == 0, 0.0, xm)
    xp = jnp.where(pos == seq_len - 1, 0.0, xp)
    return xm * w[0:1, :] + x * w[1:2, :] + xp * w[2:3, :]


def _halo_specs(tl, width, n_rows, halo=SUBLANES):
    r = tl // halo
    last = n_rows // halo - 1
    prev = pl.BlockSpec((1, halo, width), lambda b, i: (b, jnp.maximum(i * r - 1, 0), 0))
    nxt = pl.BlockSpec((1, halo, width), lambda b, i: (b, jnp.minimum((i + 1) * r, last), 0))
    return prev, nxt


def _bm(arr):
    if arr.shape[0] == 1:
        return lambda b: 0
    return lambda b: b


def _mod_kernel(c_ref, w_ref, b_ref, o_ref):
    c = c_ref[...]
    s = c * _sigmoid(c)
    w_hi, w_lo = _split2(w_ref[0])
    o_ref[0] = _dot3(s, w_hi, w_lo) + b_ref[0]


def _modulation(cond, w_mod, b_mod):
    depth, d, n = w_mod.shape
    rows = cond.shape[0]
    tn = 1024
    return pl.pallas_call(
        _mod_kernel,
        grid=(depth, n // tn),
        in_specs=[
            pl.BlockSpec((rows, d), lambda l, j: (0, 0)),
            pl.BlockSpec((1, d, tn), lambda l, j: (l, 0, j)),
            pl.BlockSpec((1, 1, tn), lambda l, j: (l, 0, j)),
        ],
        out_specs=pl.BlockSpec((1, rows, tn), lambda l, j: (l, 0, j)),
        out_shape=jax.ShapeDtypeStruct((depth, rows, n), F32),
        compiler_params=_cparams("parallel", "parallel"),
        name="modulation",
    )(cond, w_mod, b_mod.reshape(depth, 1, n))


def _proj_kernel(x_ref, sc_ref, sh_ref, w_ref, o_ref):
    h = (x_ref[0] * (1.0 + sc_ref[0]) + sh_ref[0]).astype(BF16)
    o_ref[0] = _dot(h, w_ref[...])


def _mod_proj(x, sc, sh, w, tl):
    bx, lx, d = x.shape
    n = w.shape[1]
    sel = _bm(sc)
    return pl.pallas_call(
        _proj_kernel,
        grid=(bx, lx // tl),
        in_specs=[
            pl.BlockSpec((1, tl, d), lambda b, i: (b, i, 0)),
            pl.BlockSpec((1, 1, d), lambda b, i: (sel(b), 0, 0)),
            pl.BlockSpec((1, 1, d), lambda b, i: (sel(b), 0, 0)),
            pl.BlockSpec((d, n), lambda b, i: (0, 0)),
        ],
        out_specs=pl.BlockSpec((1, tl, n), lambda b, i: (b, i, 0)),
        out_shape=jax.ShapeDtypeStruct((bx, lx, n), F32),
        compiler_params=_cparams("parallel", "parallel"),
        name="mod_proj",
    )(x, sc, sh, w)


def _store_packed(o_ref, pairs):
    for q, (xa, xb) in enumerate(pairs):
        for h in range(xa.shape[1] // HEAD):
            sl = slice(HEAD * h, HEAD * (h + 1))
            o_ref[0, h, q] = jnp.concatenate([xa[:, sl], xb[:, sl]], axis=-1)


def _rwkv_prep_kernel(p_ref, pp_ref, pn_ref, sw_ref, w0_ref, a0_ref, kkp_ref, ka_ref, rk_ref,
                      w2h_ref, w2l_ref, a2h_ref, a2l_ref, g2h_ref, g2l_ref, ones_ref,
                      pkf_o, pkb_o, bonus_o, g_o, *, seq_len, width):
    tl = p_ref.shape[1]
    i = pl.program_id(1)
    pc = _conv3(p_ref[0], pp_ref[0, SUBLANES - 1:SUBLANES, :], pn_ref[0, 0:1, :],
                sw_ref[...], i * tl, seq_len)
    w = width
    r = pc[:, 0:w]
    k = pc[:, w:2 * w]
    v = pc[:, 2 * w:3 * w]
    wd = pc[:, 3 * w:3 * w + LANES]
    ad = pc[:, 3 * w + LANES:3 * w + 2 * LANES]
    gd = pc[:, 3 * w + 2 * LANES:3 * w + 3 * LANES]
    ones_bd = ones_ref[...]

    g_o[0] = _dot3(_sigmoid(gd), g2h_ref[...], g2l_ref[...])
    kkr = k * kkp_ref[...]
    kk = kkr * lax.rsqrt(_segsum(kkr * kkr, ones_bd) + 1e-12)
    lw = _dot3(jnp.tanh(wd), w2h_ref[...], w2l_ref[...])
    la = _dot3(ad, a2h_ref[...], a2l_ref[...])

    rrk = r * rk_ref[...]
    bonus = jnp.zeros_like(r)
    for d, pk_o in enumerate((pkf_o, pkb_o)):
        z = w0_ref[d:d + 1, :] + lw[:, d * w:(d + 1) * w]
        w_log = -(jnp.maximum(-z, 0.0) + jnp.log(1.0 + jnp.exp(-jnp.abs(z)))) - 0.5
        decay = jnp.exp(-jnp.exp(w_log))
        a = _sigmoid(a0_ref[d:d + 1, :] + la[:, d * w:(d + 1) * w])
        kd = k * (1.0 + (a - 1.0) * ka_ref[...])
        _store_packed(pk_o, ((r, v), (kk, decay), (kd, kk * a)))
        bonus = bonus + _segsum(rrk * kd, ones_bd) * v
    bonus_o[0] = bonus


def _rwkv_prep(p_r, lp, tl, seq_len):
    b, l, wp = p_r.shape
    w = lp["rwkv_w"]
    nh = w // HEAD
    hm = jax.ShapeDtypeStruct((b, nh, WKV_TILES, l, LANES), F32)
    tok = jax.ShapeDtypeStruct((b, l, w), F32)
    prev, nxt = _halo_specs(tl, wp, l)

    def full(a):
        return pl.BlockSpec(a.shape, lambda b_, i: (0,) * a.ndim)

    consts = [lp["rwkv_shift"], lp["rwkv_w0"], lp["rwkv_a0"], lp["rwkv_kk"], lp["rwkv_ka"],
              lp["rwkv_rk"], lp["w2_hi"], lp["w2_lo"], lp["a2_hi"], lp["a2_lo"],
              lp["g2_hi"], lp["g2_lo"], lp["ones_head"]]
    hm_spec = pl.BlockSpec((1, nh, WKV_TILES, tl, LANES), lambda b_, i: (b_, 0, 0, i, 0))
    tok_spec = pl.BlockSpec((1, tl, w), lambda b_, i: (b_, i, 0))
    return pl.pallas_call(
        functools.partial(_rwkv_prep_kernel, seq_len=seq_len, width=w),
        grid=(b, l // tl),
        in_specs=[pl.BlockSpec((1, tl, wp), lambda b_, i: (b_, i, 0)), prev, nxt]
        + [full(a) for a in consts],
        out_specs=[hm_spec] * 2 + [tok_spec] * 2,
        out_shape=[hm] * 2 + [tok] * 2,
        compiler_params=_cparams("parallel", "parallel"),
        name="rwkv_prep",
    )(p_r, p_r, p_r, *consts)


def _allsum8(x):
    s = x[0:8]
    for q in range(1, x.shape[0] // SUBLANES):
        s = s + x[q * SUBLANES:(q + 1) * SUBLANES]
    s = s + pltpu.roll(s, 4, 0)
    s = s + pltpu.roll(s, 2, 0)
    return s + pltpu.roll(s, 1, 0)


def _wkv_kernel(pf_ref, pb_ref, s0_ref, yf_ref, yb_ref, so_ref, S, OPS, YS, ACC):
    i = pl.program_id(1)
    n = pl.num_programs(1)
    t_steps = pf_ref.shape[3]
    stride = WKV_TILES * t_steps
    pf2 = pf_ref.reshape(HEAD * stride, LANES)
    pb2 = pb_ref.reshape(HEAD * stride, LANES)
    yf2 = yf_ref.reshape(HEAD * t_steps, LANES)
    yb2 = yb_ref.reshape(HEAD * t_steps, LANES)

    @pl.when(i == 0)
    def _():
        S[...] = jnp.concatenate([s0_ref[0], s0_ref[1]], axis=0).T

    R0, V0, KK0, W0, KD0, B0, WR0 = (q * HEAD for q in range(7))

    for t in range(t_steps):
        tiles = []
        for q in range(WKV_TILES):
            f = pf2[pl.ds(q * t_steps + t, HEAD, stride=stride), :]
            b = pb2[pl.ds(q * t_steps + (t_steps - 1 - t), HEAD, stride=stride), :]
            tiles.append(jnp.concatenate([f, b], axis=0).T)
            OPS[t, q * LANES:(q + 1) * LANES, :] = tiles[q]
        OPS[t, WR0:WR0 + HEAD, :] = tiles[1][HEAD:] * tiles[0][:HEAD]

    def accumulate(j, state_tile, k, sa, y0):
        return (sa + state_tile * OPS[j, KK0 + k:KK0 + k + 1, :],
                y0 + state_tile * OPS[j, WR0 + k:WR0 + k + 1, :])

    sa = jnp.zeros((HEAD, LANES), F32)
    y0 = jnp.zeros((HEAD, LANES), F32)
    for k in range(HEAD):
        sa, y0 = accumulate(0, S[k * HEAD:(k + 1) * HEAD, :], k, sa, y0)
    ACC[0:HEAD, :] = sa
    ACC[HEAD:2 * HEAD, :] = y0

    def step(j, carry):
        jn = jnp.minimum(j + 1, t_steps - 1)
        sa = ACC[0:HEAD, :]
        y0 = ACC[HEAD:2 * HEAD, :]
        vv = OPS[j, V0:V0 + HEAD, :]
        r = OPS[j, R0:R0 + HEAD, :]
        br = jnp.tile(_allsum8(OPS[j, B0:B0 + HEAD, :] * r), (HEAD // SUBLANES, 1))
        kr = jnp.tile(_allsum8(OPS[j, KD0:KD0 + HEAD, :] * r), (HEAD // SUBLANES, 1))
        YS[j] = y0 - sa * br + vv * kr
        san = jnp.zeros((HEAD, LANES), F32)
        y0n = jnp.zeros((HEAD, LANES), F32)
        for k in range(HEAD):
            rows = slice(k * HEAD, (k + 1) * HEAD)
            snew = (S[rows, :] * OPS[j, W0 + k:W0 + k + 1, :]
                    - sa * OPS[j, B0 + k:B0 + k + 1, :]
                    + vv * OPS[j, KD0 + k:KD0 + k + 1, :])
            S[rows, :] = snew
            san, y0n = accumulate(jn, snew, k, san, y0n)
        ACC[0:HEAD, :] = san
        ACC[HEAD:2 * HEAD, :] = y0n
        return carry

    lax.fori_loop(0, t_steps, step, 0)

    zeros = jnp.zeros((HEAD, LANES), F32)
    for t in range(t_steps):
        yt = jnp.concatenate([YS[t], zeros], axis=0).T
        yf2[pl.ds(t, HEAD, stride=t_steps), :] = yt[:HEAD]
        yb2[pl.ds(t_steps - 1 - t, HEAD, stride=t_steps), :] = yt[HEAD:]

    @pl.when(i == n - 1)
    def _():
        st = S[...].T
        so_ref[0] = st[0:HEAD]
        so_ref[1] = st[HEAD:2 * HEAD]


def _wkv_scan(pkf, pkb, s0):
    b, nh, _, l, _ = pkf.shape
    gb = HEAD // nh
    assert gb * nh == HEAD and b % gb == 0 and l % SCAN_T == 0
    groups = b // gb
    n = l // SCAN_T
    pk_blk = (gb, nh, WKV_TILES, SCAN_T, LANES)
    y_blk = (gb, nh, SCAN_T, LANES)
    st = pl.BlockSpec((2, HEAD, HEAD * HEAD), lambda g, i: (0, g, 0))
    y_shape = jax.ShapeDtypeStruct((b, nh, l, LANES), F32)
    return pl.pallas_call(
        _wkv_kernel,
        grid=(groups, n),
        in_specs=[pl.BlockSpec(pk_blk, lambda g, i: (g, 0, 0, i, 0)),
                  pl.BlockSpec(pk_blk, lambda g, i: (g, 0, 0, n - 1 - i, 0)), st],
        out_specs=[pl.BlockSpec(y_blk, lambda g, i: (g, 0, i, 0)),
                   pl.BlockSpec(y_blk, lambda g, i: (g, 0, n - 1 - i, 0)), st],
        out_shape=[y_shape, y_shape, jax.ShapeDtypeStruct((2, b * nh, HEAD * HEAD), F32)],
        scratch_shapes=[pltpu.VMEM((HEAD * HEAD, LANES), F32),
                        pltpu.VMEM((SCAN_T, 7 * HEAD, LANES), F32),
                        pltpu.VMEM((SCAN_T, HEAD, LANES), F32),
                        pltpu.VMEM((2 * HEAD, LANES), F32)],
        compiler_params=_cparams("parallel", "arbitrary"),
        name="wkv_scan",
    )(pkf, pkb, s0)


def _rwkv_post_kernel(yf_ref, yb_ref, bonus_ref, g_ref, gng_ref, gnb_ref, o_ref):
    y = yf_ref[0, :, :, 0:HEAD] + yb_ref[0, :, :, 0:HEAD]
    mu = jnp.mean(y, axis=-1, keepdims=True)
    yc = y - mu
    var = jnp.mean(yc * yc, axis=-1, keepdims=True)
    yn = yc * lax.rsqrt(var + GN_EPS)
    yt = jnp.concatenate([yn[h] for h in range(yn.shape[0])], axis=-1)
    o_ref[0] = (yt * gng_ref[...] + gnb_ref[...] + bonus_ref[0]) * g_ref[0]


def _rwkv_post(yf, yb, bonus, g, gn_g, gn_b, tl):
    b, nh, l, _ = yf.shape
    w = nh * HEAD
    hm = pl.BlockSpec((1, nh, tl, LANES), lambda b_, i: (b_, 0, i, 0))
    tok = pl.BlockSpec((1, tl, w), lambda b_, i: (b_, i, 0))
    vec = pl.BlockSpec((1, w), lambda b_, i: (0, 0))
    return pl.pallas_call(
        _rwkv_post_kernel,
        grid=(b, l // tl),
        in_specs=[hm, hm, tok, tok, vec, vec],
        out_specs=tok,
        out_shape=jax.ShapeDtypeStruct((b, l, w), F32),
        compiler_params=_cparams("parallel", "parallel"),
        name="rwkv_post",
    )(yf, yb, bonus, g, gn_g, gn_b)


def _hy_prep_kernel(p_ref, pp_ref, pn_ref, sw_ref, x0_o, u_o, ub_o, *, seq_len, width):
    tl = p_ref.shape[1]
    i = pl.program_id(1)
    pc = _conv3(p_ref[0], pp_ref[0, SUBLANES - 1:SUBLANES, :], pn_ref[0, 0:1, :],
                sw_ref[...], i * tl, seq_len)
    w = width
    x0_o[0] = pc[:, 0:w]
    u = pc[:, w:2 * w] * pc[:, 2 * w:3 * w]
    u_o[0] = u
    ub_o[0] = u.astype(BF16)


def _hy_prep(p_h, short_w, tl, seq_len):
    b, l, w3 = p_h.shape
    w = w3 // 3
    prev, nxt = _halo_specs(tl, w3, l)
    tok = pl.BlockSpec((1, tl, w), lambda b_, i: (b_, i, 0))
    return pl.pallas_call(
        functools.partial(_hy_prep_kernel, seq_len=seq_len, width=w),
        grid=(b, l // tl),
        in_specs=[pl.BlockSpec((1, tl, w3), lambda b_, i: (b_, i, 0)), prev, nxt,
                  pl.BlockSpec((3, w3), lambda b_, i: (0, 0))],
        out_specs=[tok, tok, tok],
        out_shape=[jax.ShapeDtypeStruct((b, l, w), F32), jax.ShapeDtypeStruct((b, l, w), F32),
                   jax.ShapeDtypeStruct((b, l, w), BF16)],
        compiler_params=_cparams("parallel", "parallel"),
        name="hy_prep",
    )(p_h, p_h, p_h, short_w)


def _hy_mlp_kernel(z_ref, w1_ref, b1_ref, f_ref, w2_ref, b2_ref, w3_ref, dec_ref, hh_o, hl_o):
    z = z_ref[...]
    t01 = z[:, 0:1]
    w1h, w1l = _split2(w1_ref[...])
    w2h, w2l = _split2(w2_ref[...])
    w3h, w3l = _split2(w3_ref[...])
    h = jnp.sin(f_ref[0:1, :] * (_dot3(z, w1h, w1l) + b1_ref[...]))
    h = jnp.sin(f_ref[1:2, :] * (_dot3(h, w2h, w2l) + b2_ref[...]))
    h = _dot3(h, w3h, w3l) * jnp.exp(-t01 * jnp.abs(dec_ref[...]))
    hi, lo = _split2(h)
    hh_o[0] = hi
    hl_o[0] = lo


def _hy_mlp(z, lp, tl):
    n = z.shape[0]
    c2 = lp["hy_w3"].shape[1]

    def full(a):
        return pl.BlockSpec(a.shape, lambda i: (0,) * a.ndim)

    consts = [lp["hy_w1"], lp["hy_b1"], lp["hy_freq"], lp["hy_w2"], lp["hy_b2"], lp["hy_w3"],
              lp["hy_decay"]]
    out = pl.BlockSpec((1, tl, c2), lambda i: (0, i, 0))
    return pl.pallas_call(
        _hy_mlp_kernel,
        grid=(n // tl,),
        in_specs=[pl.BlockSpec((tl, LANES), lambda i: (i, 0))] + [full(a) for a in consts],
        out_specs=[out, out],
        out_shape=[jax.ShapeDtypeStruct((1, n, c2), BF16)] * 2,
        compiler_params=_cparams("parallel"),
        name="hy_mlp",
    )(z, *consts)


def _dft3(fh_ref, fl_ref, xh, xl):
    fh = fh_ref[...]
    return _dot(fh, xh) + (_dot(fl_ref[...], xh) + _dot(fh, xl))


def _dft_filter_kernel(frh, frl, fih, fil, xh_ref, xl_ref, hre_o, him_o):
    xh, xl = xh_ref[0], xl_ref[0]
    w = xh.shape[1] // 2
    xre = _dft3(frh, frl, xh, xl)
    xim = _dft3(fih, fil, xh, xl)
    hre_o[...] = xre[:, :w] + xre[:, w:]
    row = lax.broadcasted_iota(jnp.int32, (xre.shape[0], 1), 0) + pl.program_id(0) * xre.shape[0]
    him_o[...] = jnp.where(row == 0, xim[:, :w] + xim[:, w:], xim[:, :w] - xim[:, w:])


def _dft_filter(mats, hh, hl, tf):
    n = hh.shape[1]
    w = hh.shape[2] // 2
    ft = pl.BlockSpec((tf, n), lambda i: (i, 0))
    xs = pl.BlockSpec((1, n, 2 * w), lambda i: (0, 0, 0))
    out = pl.BlockSpec((tf, w), lambda i: (i, 0))
    return pl.pallas_call(
        _dft_filter_kernel,
        grid=(n // tf,),
        in_specs=[ft, ft, ft, ft, xs, xs],
        out_specs=[out, out],
        out_shape=[jax.ShapeDtypeStruct((n, w), F32)] * 2,
        compiler_params=_cparams("parallel"),
        name="dft_filter",
    )(mats["fre_hi"], mats["fre_lo"], mats["fim_hi"], mats["fim_lo"], hh, hl)


def _dft_signal_kernel(fr_ref, fi_ref, x_ref, hre_ref, him_ref, pr_o, pi_o):
    x = x_ref[0]
    xre = _dot(fr_ref[...], x)
    xim = _dot(fi_ref[...], x)
    hre, him = hre_ref[...], him_ref[...]
    row = lax.broadcasted_iota(jnp.int32, (xre.shape[0], 1), 0) + pl.program_id(1) * xre.shape[0]
    pr_o[0] = jnp.where(row == 0, xre * hre, xre * hre - xim * him).astype(BF16)
    pi_o[0] = jnp.where(row == 0, xim * him, xre * him + xim * hre).astype(BF16)


def _dft_signal(mats, ub, hre, him, tf):
    b, n, w = ub.shape
    ft = pl.BlockSpec((tf, n), lambda b_, i: (i, 0))
    xs = pl.BlockSpec((1, n, w), lambda b_, i: (b_, 0, 0))
    hs = pl.BlockSpec((tf, w), lambda b_, i: (i, 0))
    out = pl.BlockSpec((1, tf, w), lambda b_, i: (b_, i, 0))
    return pl.pallas_call(
        _dft_signal_kernel,
        grid=(b, n // tf),
        in_specs=[ft, ft, xs, hs, hs],
        out_specs=[out] * 2,
        out_shape=[jax.ShapeDtypeStruct((b, n, w), BF16)] * 2,
        compiler_params=_cparams("parallel", "parallel"),
        name="dft_signal",
    )(mats["fre"], mats["fim"], ub, hre, him)


def _dft_inverse_kernel(gr_ref, gi_ref, pr_ref, pi_ref, x0_ref, u_ref, bias_ref, o_ref):
    y = _dot(gr_ref[...], pr_ref[0]) + _dot(gi_ref[...], pi_ref[0])
    o_ref[0] = x0_ref[0] * (y + u_ref[0] * bias_ref[...])


def _dft_inverse(mats, pr, pi, x0, u, bias, tt):
    b, n, w = u.shape
    gt = pl.BlockSpec((tt, n), lambda b_, i: (i, 0))
    ps = pl.BlockSpec((1, n, w), lambda b_, i: (b_, 0, 0))
    tok = pl.BlockSpec((1, tt, w), lambda b_, i: (b_, i, 0))
    return pl.pallas_call(
        _dft_inverse_kernel,
        grid=(b, n // tt),
        in_specs=[gt, gt, ps, ps, tok, tok, pl.BlockSpec((1, w), lambda b_, i: (0, 0))],
        out_specs=tok,
        out_shape=jax.ShapeDtypeStruct((b, n, w), F32),
        compiler_params=_cparams("parallel", "parallel"),
        name="dft_inverse",
    )(mats["gre"], mats["gim"], pr, pi, x0, u, bias)


def _swap_halves(x):
    lane = lax.broadcasted_iota(jnp.int32, x.shape, 1)
    half = HEAD // 2
    return jnp.where(lane % HEAD < half, pltpu.roll(x, LANES - half, 1), pltpu.roll(x, half, 1))


def _attn_prep_kernel(*refs, rope, wq, wkv):
    if rope:
        p_ref, qn_ref, kn_ref, ones_ref, cos_ref, sin_ref, q_o, k_o, v_o, kt_o = refs
    else:
        p_ref, qn_ref, kn_ref, ones_ref, q_o, k_o, v_o, kt_o = refs
    ones_bd = ones_ref[...]

    def norm_rope(x, gain):
        ms = _segsum(x * x, ones_bd) * (1.0 / HEAD)
        xn = x * lax.rsqrt(ms + QK_EPS) * gain
        if rope:
            xn = xn * cos_ref[...] + _swap_halves(xn) * sin_ref[...]
        return xn

    scale = HEAD ** -0.5
    for c in range(wq // LANES):
        xq = norm_rope(p_ref[0, :, c * LANES:(c + 1) * LANES], qn_ref[...]) * scale
        q_o[0, 2 * c] = xq[:, :HEAD].astype(BF16)
        q_o[0, 2 * c + 1] = xq[:, HEAD:].astype(BF16)
    for c in range(wkv // LANES):
        x = p_ref[0, :, wq + c * LANES:wq + (c + 1) * LANES]
        ms = _segsum(x * x, ones_bd) * (1.0 / HEAD)
        xk = x * lax.rsqrt(ms + QK_EPS) * kn_ref[...]
        kt_o[0, :, c * LANES:(c + 1) * LANES] = xk
        if rope:
            xk = xk * cos_ref[...] + _swap_halves(xk) * sin_ref[...]
        k_o[0, 2 * c] = xk[:, :HEAD].astype(BF16)
        k_o[0, 2 * c + 1] = xk[:, HEAD:].astype(BF16)
        xv = p_ref[0, :, wq + wkv + c * LANES:wq + wkv + (c + 1) * LANES]
        v_o[0, 2 * c] = xv[:, :HEAD].astype(BF16)
        v_o[0, 2 * c + 1] = xv[:, HEAD:].astype(BF16)


def _attn_prep(p_a, lp, rope_tabs, tl, wq, wkv):
    b, l, wa = p_a.shape
    nq, nkv = wq // HEAD, wkv // HEAD
    rope = rope_tabs is not None
    vec = pl.BlockSpec((1, LANES), lambda b_, i: (0, 0))
    in_specs = [pl.BlockSpec((1, tl, wa), lambda b_, i: (b_, i, 0)), vec, vec,
                pl.BlockSpec((LANES, LANES), lambda b_, i: (0, 0))]
    args = [p_a, lp["attn_qn"], lp["attn_kn"], lp["ones_pair"]]
    if rope:
        tab = pl.BlockSpec((tl, LANES), lambda b_, i: (i, 0))
        in_specs += [tab, tab]
        args += list(rope_tabs)
    return pl.pallas_call(
        functools.partial(_attn_prep_kernel, rope=rope, wq=wq, wkv=wkv),
        grid=(b, l // tl),
        in_specs=in_specs,
        out_specs=[pl.BlockSpec((1, nq, tl, HEAD), lambda b_, i: (b_, 0, i, 0)),
                   pl.BlockSpec((1, nkv, tl, HEAD), lambda b_, i: (b_, 0, i, 0)),
                   pl.BlockSpec((1, nkv, tl, HEAD), lambda b_, i: (b_, 0, i, 0)),
                   pl.BlockSpec((1, tl, wkv), lambda b_, i: (b_, i, 0))],
        out_shape=[jax.ShapeDtypeStruct((b, nq, l, HEAD), BF16),
                   jax.ShapeDtypeStruct((b, nkv, l, HEAD), BF16),
                   jax.ShapeDtypeStruct((b, nkv, l, HEAD), BF16),
                   jax.ShapeDtypeStruct((b, l, wkv), F32)],
        compiler_params=_cparams("parallel", "parallel"),
        name="attn_prep",
    )(*args)


def _attn_kernel(q_ref, k_ref, v_ref, o_ref):
    k, v = k_ref[0, 0], v_ref[0, 0]
    outs = []
    for h in range(q_ref.shape[1]):
        s = lax.dot_general(q_ref[0, h], k, (((1,), (1,)), ((), ())), preferred_element_type=F32)
        m = jnp.max(s, axis=-1, keepdims=True)
        p = jnp.exp(s - m)
        den = jnp.sum(p, axis=-1, keepdims=True)
        outs.append(_dot(p.astype(BF16), v) / den)
    o_ref[0] = jnp.concatenate(outs, axis=-1)


def _attention(q, k, v, tq):
    b, nq, l, _ = q.shape
    nkv, lk = k.shape[1], k.shape[2]
    g = nq // nkv
    kv = pl.BlockSpec((1, 1, lk, HEAD), lambda b_, h, i: (b_, h, 0, 0))
    return pl.pallas_call(
        _attn_kernel,
        grid=(b, nkv, l // tq),
        in_specs=[pl.BlockSpec((1, g, tq, HEAD), lambda b_, h, i: (b_, h, i, 0)), kv, kv],
        out_specs=pl.BlockSpec((1, tq, g * HEAD), lambda b_, h, i: (b_, i, h)),
        out_shape=jax.ShapeDtypeStruct((b, l, nq * HEAD), F32),
        compiler_params=_cparams("parallel", "parallel", "parallel"),
        name="attention",
    )(q, k, v)


def _out_proj_kernel(yr_ref, yh_ref, ya_ref, x_ref, g_ref, w_ref, lng_ref, lnb_ref, o_ref, *, alpha):
    wr, wh = yr_ref.shape[2], yh_ref.shape[2]
    mix = _dot(yr_ref[0].astype(BF16), w_ref[0:wr, :])
    mix = mix + _dot(yh_ref[0].astype(BF16), w_ref[wr:wr + wh, :])
    mix = mix + _dot(ya_ref[0].astype(BF16), w_ref[wr + wh:, :])
    o_ref[0] = _layer_norm(alpha * x_ref[0] + g_ref[0] * mix, lng_ref[...], lnb_ref[...])


def _out_proj(y_r, y_h, y_a, x, gate, w_out, ln_g, ln_b, tl, alpha):
    bx, lx, d = x.shape
    sel = _bm(gate)

    def tok(a):
        return pl.BlockSpec((1, tl, a.shape[2]), lambda b, i: (b, i, 0))

    vec = pl.BlockSpec((1, d), lambda b, i: (0, 0))
    return pl.pallas_call(
        functools.partial(_out_proj_kernel, alpha=alpha),
        grid=(bx, lx // tl),
        in_specs=[tok(y_r), tok(y_h), tok(y_a), tok(x),
                  pl.BlockSpec((1, 1, d), lambda b, i: (sel(b), 0, 0)),
                  pl.BlockSpec(w_out.shape, lambda b, i: (0, 0)), vec, vec],
        out_specs=tok(x),
        out_shape=jax.ShapeDtypeStruct((bx, lx, d), F32),
        compiler_params=_cparams("parallel", "parallel"),
        name="out_proj",
    )(y_r, y_h, y_a, x, gate, w_out, ln_g, ln_b)


def _ffn_kernel(x_ref, xp_ref, xn_ref, sc_ref, sh_ref, g_ref, wa_ref, wb_ref, ca_ref, cb_ref,
                wd_ref, lng_ref, lnb_ref, o_ref, h_s, acc_s, *, seq_len, alpha):
    tl = x_ref.shape[1]
    i = pl.program_id(1)
    j = pl.program_id(2)
    halo = FFN_HALO

    @pl.when(j == 0)
    def _():
        sc, sh = 1.0 + sc_ref[0], sh_ref[0]
        h_s[0:halo] = (xp_ref[0] * sc + sh).astype(BF16)
        h_s[halo:halo + tl] = (x_ref[0] * sc + sh).astype(BF16)
        h_s[halo + tl:] = (xn_ref[0] * sc + sh).astype(BF16)
        acc_s[...] = jnp.zeros_like(acc_s)

    row = lax.broadcasted_iota(jnp.int32, (tl, 1), 0)
    pos = (i * tl + row) % seq_len
    first = pos == 0
    last = pos == seq_len - 1
    h = h_s[...]

    def conv_up(w_ref, c_ref):
        u = _dot(h, w_ref[...])
        um = jnp.where(first, 0.0, pltpu.roll(u, 1, 0)[halo:halo + tl])
        up = jnp.where(last, 0.0, pltpu.roll(u, tl + 2 * halo - 1, 0)[halo:halo + tl])
        return um * c_ref[0:1, :] + u[halo:halo + tl] * c_ref[1:2, :] + up * c_ref[2:3, :]

    a = conv_up(wa_ref, ca_ref)
    b = conv_up(wb_ref, cb_ref)
    f = (a * _sigmoid(a) * b).astype(BF16)
    acc_s[...] += _dot(f, wd_ref[...])

    @pl.when(j == pl.num_programs(2) - 1)
    def _():
        o_ref[0] = _layer_norm(alpha * x_ref[0] + g_ref[0] * acc_s[...], lng_ref[...], lnb_ref[...])


def _ffn(x, sc, sh, gate, w_up, conv_w, w_down, ln_g, ln_b, tl, tn, seq_len, alpha):
    bx, lx, d = x.shape
    dff = w_down.shape[0]
    nj = dff // tn
    sel = _bm(sc)
    prev, nxt = _halo_specs(tl, d, lx, FFN_HALO)
    prev3 = pl.BlockSpec(prev.block_shape, lambda b, i, j: prev.index_map(b, i))
    nxt3 = pl.BlockSpec(nxt.block_shape, lambda b, i, j: nxt.index_map(b, i))
    mod = pl.BlockSpec((1, 1, d), lambda b, i, j: (sel(b), 0, 0))
    vec = pl.BlockSpec((1, d), lambda b, i, j: (0, 0))
    tok = pl.BlockSpec((1, tl, d), lambda b, i, j: (b, i, 0))
    return pl.pallas_call(
        functools.partial(_ffn_kernel, seq_len=seq_len, alpha=alpha),
        grid=(bx, lx // tl, nj),
        in_specs=[tok, prev3, nxt3, mod, mod, mod,
                  pl.BlockSpec((d, tn), lambda b, i, j: (0, j)),
                  pl.BlockSpec((d, tn), lambda b, i, j: (0, j + nj)),
                  pl.BlockSpec((3, tn), lambda b, i, j: (0, j)),
                  pl.BlockSpec((3, tn), lambda b, i, j: (0, j + nj)),
                  pl.BlockSpec((tn, d), lambda b, i, j: (j, 0)), vec, vec],
        out_specs=tok,
        out_shape=jax.ShapeDtypeStruct((bx, lx, d), F32),
        scratch_shapes=[pltpu.VMEM((tl + 2 * FFN_HALO, d), BF16), pltpu.VMEM((tl, d), F32)],
        compiler_params=_cparams("parallel", "parallel", "arbitrary"),
        name="conv_ffn",
    )(x, x, x, sc, sh, gate, w_up, w_up, conv_w, conv_w, w_down, ln_g, ln_b)


def _split2_host(x):
    hi = lax.bitcast_convert_type(lax.bitcast_convert_type(x, jnp.uint32) & jnp.uint32(0xFFFF0000), F32)
    return hi.astype(BF16), (x - hi).astype(BF16)


def _dft_mats(n):
    big = 2 * n
    k = jnp.arange(n, dtype=jnp.int32)[:, None]
    t = jnp.arange(n, dtype=jnp.int32)[None, :]
    ang = ((k * t) % big).astype(F32) * (2.0 * math.pi / big)
    fre = jnp.cos(ang)
    fim = jnp.where(k == 0, jnp.where(t % 2 == 0, 1.0, -1.0), -jnp.sin(ang))
    scale = jnp.where(k == 0, 1.0 / big, 2.0 / big)
    out = {"fre": fre.astype(BF16), "fim": fim.astype(BF16),
           "gre": (fre * scale).T.astype(BF16), "gim": (fim * scale).T.astype(BF16)}
    for name, m in (("fre", fre), ("fim", fim)):
        out[name + "_hi"], out[name + "_lo"] = _split2_host(m)
    return out


def _hyena_features(n, n_bands):
    t01 = jnp.linspace(0.0, 1.0, n, dtype=F32)[:, None]
    pos = jnp.arange(n, dtype=F32)[:, None]
    bands = jnp.linspace(1e-4, n_bands - 1, n_bands, dtype=F32)[None, :]
    ang = (2.0 * math.pi / n) * pos * bands
    z = jnp.concatenate([t01, jnp.cos(ang), -jnp.sin(ang)], -1)
    return jnp.pad(z, ((0, 0), (0, LANES - z.shape[1])))


def _rope_tables(n_tokens):
    rows = n_tokens // GRID_W
    row = jnp.repeat(jnp.arange(rows, dtype=F32), GRID_W)
    col = jnp.tile(jnp.arange(GRID_W, dtype=F32), rows)
    n_freq = HEAD // 4
    inv = ROPE_THETA ** (-jnp.arange(n_freq, dtype=F32) / n_freq)
    ang = jnp.concatenate([row[:, None] * inv, col[:, None] * inv], -1)
    cos, sin = jnp.cos(ang), jnp.sin(ang)
    cos2 = jnp.tile(jnp.concatenate([cos, cos], -1), (1, LANES // HEAD))
    sin2 = jnp.tile(jnp.concatenate([-sin, sin], -1), (1, LANES // HEAD))
    return cos2, sin2


def _block_ones(n, group):
    idx = jnp.arange(n) // group
    return (idx[:, None] == idx[None, :]).astype(BF16)


def _pad_to(a, shape):
    return jnp.pad(a, [(0, s - d) for d, s in zip(a.shape, shape)])


def _block_diag2(a, b):
    za = jnp.zeros((a.shape[0], b.shape[1]), a.dtype)
    zb = jnp.zeros((b.shape[0], a.shape[1]), a.dtype)
    return jnp.concatenate([jnp.concatenate([a, za], 1), jnp.concatenate([zb, b], 1)], 0)


def _layer_params(l, P, dims):
    w_r, w_h, wq, wkv = dims["w_rwkv"], dims["w_hyena"], dims["wq"], dims["wkv"]
    rwkv_cols = 3 * w_r + 2 * P["rwkv_w2"].shape[2] + 2 * P["rwkv_a2"].shape[2] + P["rwkv_g2"].shape[1]
    rwkv_pad = -(-rwkv_cols // (2 * LANES)) * (2 * LANES)
    hy_cols = 3 * w_h
    w_in = P["w_in"][l]
    lp = {"rwkv_w": w_r}
    lp["w_in_r"] = _pad_to(w_in[:, :rwkv_cols], (w_in.shape[0], rwkv_pad)).astype(BF16)
    lp["w_in_h"] = w_in[:, rwkv_cols:rwkv_cols + hy_cols].astype(BF16)
    lp["w_in_a"] = w_in[:, rwkv_cols + hy_cols:].astype(BF16)
    lp["rwkv_shift"] = _pad_to(P["rwkv_shift"][l], (3, rwkv_pad))
    lp["rwkv_w0"] = P["rwkv_w0"][l]
    lp["rwkv_a0"] = P["rwkv_a0"][l]
    for nm in ("rwkv_kk", "rwkv_ka", "rwkv_rk", "rwkv_gn_g", "rwkv_gn_b", "hy_bias",
               "ln1_g", "ln1_b", "ln2_g", "ln2_b"):
        lp[nm] = P[nm][l][None, :]
    lp["w2_hi"], lp["w2_lo"] = _split2_host(_block_diag2(P["rwkv_w2"][l, 0], P["rwkv_w2"][l, 1]))
    lp["a2_hi"], lp["a2_lo"] = _split2_host(_block_diag2(P["rwkv_a2"][l, 0], P["rwkv_a2"][l, 1]))
    lp["g2_hi"], lp["g2_lo"] = _split2_host(P["rwkv_g2"][l])
    lp["ones_head"] = _block_ones(w_r, HEAD)
    lp["ones_pair"] = _block_ones(LANES, HEAD)
    lp["hy_short"] = P["hy_short"][l]
    ffn_w = P["hy_w1"].shape[2]
    lp["hy_w1"] = _pad_to(P["hy_w1"][l], (LANES, LANES))
    lp["hy_b1"] = _pad_to(P["hy_b1"][l][None, :], (1, LANES))
    lp["hy_freq"] = _pad_to(P["hy_freq"][l], (2, LANES))
    lp["hy_w2"] = _pad_to(P["hy_w2"][l], (LANES, LANES))
    lp["hy_b2"] = _pad_to(P["hy_b2"][l][None, :], (1, LANES))
    lp["hy_w3"] = _pad_to(P["hy_w3"][l], (LANES, 2 * w_h))
    lp["hy_decay"] = P["hy_decay"][l].reshape(1, 2 * w_h)
    del ffn_w
    lp["attn_qn"] = jnp.tile(P["attn_qn"][l], LANES // HEAD)[None, :]
    lp["attn_kn"] = jnp.tile(P["attn_kn"][l], LANES // HEAD)[None, :]
    lp["w_out"] = P["w_out"][l].astype(BF16)
    lp["ffn_up"] = P["ffn_up"][l].astype(BF16)
    lp["ffn_conv"] = P["ffn_conv"][l]
    lp["ffn_down"] = P["ffn_down"][l].astype(BF16)
    return lp


def _trunk_layer(x, mod6, lp, dims, rope_tabs, ctx_kv, s0, dft, z_feat, tiles):
    b, l, d = x.shape
    sh1, sc1, g1, sh2, sc2, g2 = mod6
    shared = sh1.shape[0] == 1
    alpha = dims["alpha"]
    w_r, w_h, wq, wkv = dims["w_rwkv"], dims["w_hyena"], dims["wq"], dims["wkv"]
    nh = w_r // HEAD
    tl_mm, tl_ew = tiles["mm"], tiles["ew"]

    xm = x.reshape(1, b * l, d) if shared else x
    p_r = _mod_proj(xm, sc1, sh1, lp["w_in_r"], tl_mm).reshape(b, l, -1)
    p_h = _mod_proj(xm, sc1, sh1, lp["w_in_h"], tl_mm).reshape(b, l, -1)
    p_a = _mod_proj(xm, sc1, sh1, lp["w_in_a"], tl_mm).reshape(b, l, -1)

    pkf, pkb, bonus, gate = _rwkv_prep(p_r, lp, tl_ew, l)
    yf, yb, s_fin = _wkv_scan(pkf, pkb, s0)
    y_r = _rwkv_post(yf, yb, bonus, gate, lp["rwkv_gn_g"], lp["rwkv_gn_b"], tl_ew)

    x0, u, ub = _hy_prep(p_h, lp["hy_short"], tl_ew, l)
    hh, hl = _hy_mlp(z_feat, lp, min(l, 256))
    tf = min(l, 256)
    hre, him = _dft_filter(dft, hh, hl, tf)
    pr, pi = _dft_signal(dft, ub, hre, him, tf)
    y_h = _dft_inverse(dft, pr, pi, x0, u, lp["hy_bias"], tf)

    q, k, vv, k_tok = _attn_prep(p_a, lp, rope_tabs, tl_ew, wq, wkv)
    if ctx_kv is not None:
        ck, cv = ctx_kv
        k = jnp.concatenate([k, jnp.swapaxes(ck, 1, 2).astype(BF16)], axis=2)
        vv = jnp.concatenate([vv, jnp.swapaxes(cv, 1, 2).astype(BF16)], axis=2)
    y_a = _attention(q, k, vv, tiles["tq"])

    def m(a):
        return a.reshape(1, b * l, a.shape[-1]) if shared else a

    x1 = _out_proj(m(y_r), m(y_h), m(y_a), xm, g1, lp["w_out"], lp["ln1_g"], lp["ln1_b"],
                   tl_mm, alpha)
    x2 = _ffn(x1, sc2, sh2, g2, lp["ffn_up"], lp["ffn_conv"], lp["ffn_down"],
              lp["ln2_g"], lp["ln2_b"], tl_mm, tiles["ffn_tn"], l, alpha)
    v_tok = p_a[..., wq + wkv:]
    return x2.reshape(b, l, d), k_tok, v_tok, s_fin


def kernel(x_prompt, x_sample, cache_k, cache_v, state_rwkv, c, c_ctx, w_mod, b_mod, w_in, rwkv_shift, rwkv_w0, rwkv_w2, rwkv_a0, rwkv_a2, rwkv_kk, rwkv_ka, rwkv_rk, rwkv_g2, rwkv_gn_g, rwkv_gn_b, hy_short, hy_w1, hy_b1, hy_freq, hy_w2, hy_b2, hy_w3, hy_decay, hy_bias, attn_qn, attn_kn, w_out, ln1_g, ln1_b, ln2_g, ln2_b, ffn_up, ffn_conv, ffn_down):
    P = dict(w_in=w_in, rwkv_shift=rwkv_shift, rwkv_w0=rwkv_w0, rwkv_w2=rwkv_w2, rwkv_a0=rwkv_a0,
             rwkv_a2=rwkv_a2, rwkv_kk=rwkv_kk, rwkv_ka=rwkv_ka, rwkv_rk=rwkv_rk, rwkv_g2=rwkv_g2,
             rwkv_gn_g=rwkv_gn_g, rwkv_gn_b=rwkv_gn_b, hy_short=hy_short, hy_w1=hy_w1, hy_b1=hy_b1,
             hy_freq=hy_freq, hy_w2=hy_w2, hy_b2=hy_b2, hy_w3=hy_w3, hy_decay=hy_decay,
             hy_bias=hy_bias, attn_qn=attn_qn, attn_kn=attn_kn, w_out=w_out, ln1_g=ln1_g,
             ln1_b=ln1_b, ln2_g=ln2_g, ln2_b=ln2_b, ffn_up=ffn_up, ffn_conv=ffn_conv,
             ffn_down=ffn_down)
    depth, d = w_mod.shape[0], w_mod.shape[1]
    bc, lc, _ = x_prompt.shape
    bd, ld, _ = x_sample.shape
    nkv = cache_k.shape[3]
    nh = state_rwkv.shape[3]
    w_r = nh * HEAD
    w_h = hy_bias.shape[1]
    wkv = nkv * HEAD
    wq = w_in.shape[2] - (3 * w_r + 2 * rwkv_w2.shape[2] + 2 * rwkv_a2.shape[2] + rwkv_g2.shape[1]) \
        - 3 * w_h - 2 * wkv
    dims = dict(w_rwkv=w_r, w_hyena=w_h, wq=wq, wkv=wkv, alpha=(2 * depth) ** 0.25)

    rows = -(-(bd + 1) // SUBLANES) * SUBLANES
    cond = _pad_to(jnp.concatenate([c, c_ctx[None, :]], 0), (rows, d))
    mod = _modulation(cond, w_mod, b_mod)

    rope_tabs = _rope_tables(ld)
    dft_c, dft_d = _dft_mats(lc), _dft_mats(ld)
    n_bands = (hy_w1.shape[1] - 1) // 2
    z_c, z_d = _hyena_features(lc, n_bands), _hyena_features(ld, n_bands)
    tiles_c = dict(mm=min(512, bc * lc), ew=min(256, lc), tq=min(256, lc), ffn_tn=512)
    tiles_d = dict(mm=min(512, ld), ew=min(256, ld), tq=min(512, ld), ffn_tn=512)

    xp, xs = x_prompt, x_sample
    zero_state = jnp.zeros((2, bc * nh, HEAD * HEAD), F32)
    new_k, new_v, new_s = [], [], []
    for l in range(depth):
        lp = _layer_params(l, P, dims)
        mod_d = [m[:bd, None, :] for m in jnp.split(mod[l], 6, axis=-1)]
        mod_c = [m[bd:bd + 1, None, :] for m in jnp.split(mod[l], 6, axis=-1)]
        xp, k_c, v_c, s_c = _trunk_layer(xp, mod_c, lp, dims, None, None, zero_state, dft_c, z_c, tiles_c)
        new_k.append(k_c.reshape(bc, lc, nkv, HEAD))
        new_v.append(v_c.reshape(bc, lc, nkv, HEAD))
        new_s.append(jnp.transpose(s_c.reshape(2, bc, nh, HEAD, HEAD), (1, 0, 2, 4, 3)))
        s0 = jnp.transpose(state_rwkv[:, l], (1, 0, 2, 4, 3)).reshape(2, bd * nh, HEAD * HEAD)
        xs, _, _, _ = _trunk_layer(xs, mod_d, lp, dims, rope_tabs, (cache_k[:, l], cache_v[:, l]),
                                   s0, dft_d, z_d, tiles_d)
    return (xp, xs, jnp.stack(new_k, axis=1), jnp.stack(new_v, axis=1), jnp.stack(new_s, axis=1))
```

```python
import functools
import math

import jax
import jax.numpy as jnp
from jax import lax
from jax.experimental import pallas as pl
from jax.experimental.pallas import tpu as pltpu

F32 = jnp.float32
BF16 = jnp.bfloat16

HEAD = 64
LANES = 128
SUBLANES = 8
VMEM_LIMIT = 52 * 1024 * 1024
LN_EPS = 1e-5
QK_EPS = 1e-6
GN_EPS = 64e-5
ROPE_THETA = 10000.0
GRID_W = 64
SCAN_T = 16
WKV_TILES = 3
FFN_HALO = 16


def _cparams(*sem):
    return pltpu.CompilerParams(dimension_semantics=sem, vmem_limit_bytes=VMEM_LIMIT)


def _dot(a, b):
    return jnp.dot(a, b, preferred_element_type=F32)


def _hi_f32(x):
    u = pltpu.bitcast(x, jnp.uint32) & jnp.uint32(0xFFFF0000)
    return pltpu.bitcast(u, F32)


def _split2(x):
    h = _hi_f32(x)
    return h.astype(BF16), (x - h).astype(BF16)


def _split3(x):
    h1 = _hi_f32(x)
    r1 = x - h1
    h2 = _hi_f32(r1)
    return h1.astype(BF16), h2.astype(BF16), (r1 - h2).astype(BF16)


def _dot3(a, b_hi, b_lo):
    a_hi, a_lo = _split2(a)
    return _dot(a_hi, b_hi) + (_dot(a_lo, b_hi) + _dot(a_hi, b_lo))


def _segsum(x, ones_bd):
    h1, h2, h3 = _split3(x)
    return _dot(h1, ones_bd) + (_dot(h2, ones_bd) + _dot(h3, ones_bd))


def _layer_norm(x, g, b):
    mu = jnp.mean(x, axis=-1, keepdims=True)
    xc = x - mu
    var = jnp.mean(xc * xc, axis=-1, keepdims=True)
    return xc * lax.rsqrt(var + LN_EPS) * g + b


def _sigmoid(x):
    return 1.0 / (1.0 + jnp.exp(-x))


def _conv3(x, prev_row, next_row, w, row0, seq_len):
    tl = x.shape[0]
    row = lax.broadcasted_iota(jnp.int32, (tl, 1), 0)
    pos = (row0 + row) % seq_len
    xm = jnp.where(row == 0, prev_row, pltpu.roll(x, 1, 0))
    xp = jnp.where(row == tl - 1, next_row, pltpu.roll(x, tl - 1, 0))
    xm = jnp.where(pos == 0, 0.0, xm)
    xp = jnp.where(pos == seq_len - 1, 0.0, xp)
    return xm * w[0:1, :] + x * w[1:2, :] + xp * w[2:3, :]


def _halo_specs(tl, width, n_rows, halo=SUBLANES):
    r = tl // halo
    last = n_rows // halo - 1
    prev = pl.BlockSpec((1, halo, width), lambda b, i: (b, jnp.maximum(i * r - 1, 0), 0))
    nxt = pl.BlockSpec((1, halo, width), lambda b, i: (b, jnp.minimum((i + 1) * r, last), 0))
    return prev, nxt


def _bm(arr):
    if arr.shape[0] == 1:
        return lambda b: 0
    return lambda b: b


def _mod_kernel(c_ref, w_ref, b_ref, o_ref):
    c = c_ref[...]
    s = c * _sigmoid(c)
    w_hi, w_lo = _split2(w_ref[0])
    o_ref[0] = _dot3(s, w_hi, w_lo) + b_ref[0]


def _modulation(cond, w_mod, b_mod):
    depth, d, n = w_mod.shape
    rows = cond.shape[0]
    tn = 1024
    return pl.pallas_call(
        _mod_kernel,
        grid=(depth, n // tn),
        in_specs=[
            pl.BlockSpec((rows, d), lambda l, j: (0, 0)),
            pl.BlockSpec((1, d, tn), lambda l, j: (l, 0, j)),
            pl.BlockSpec((1, 1, tn), lambda l, j: (l, 0, j)),
        ],
        out_specs=pl.BlockSpec((1, rows, tn), lambda l, j: (l, 0, j)),
        out_shape=jax.ShapeDtypeStruct((depth, rows, n), F32),
        compiler_params=_cparams("parallel", "parallel"),
        name="modulation",
    )(cond, w_mod, b_mod.reshape(depth, 1, n))


def _proj_kernel(x_ref, sc_ref, sh_ref, w_ref, o_ref):
    h = (x_ref[0] * (1.0 + sc_ref[0]) + sh_ref[0]).astype(BF16)
    o_ref[0] = _dot(h, w_ref[...])


def _mod_proj(x, sc, sh, w, tl):
    bx, lx, d = x.shape
    n = w.shape[1]
    sel = _bm(sc)
    return pl.pallas_call(
        _proj_kernel,
        grid=(bx, lx // tl),
        in_specs=[
            pl.BlockSpec((1, tl, d), lambda b, i: (b, i, 0)),
            pl.BlockSpec((1, 1, d), lambda b, i: (sel(b), 0, 0)),
            pl.BlockSpec((1, 1, d), lambda b, i: (sel(b), 0, 0)),
            pl.BlockSpec((d, n), lambda b, i: (0, 0)),
        ],
        out_specs=pl.BlockSpec((1, tl, n), lambda b, i: (b, i, 0)),
        out_shape=jax.ShapeDtypeStruct((bx, lx, n), F32),
        compiler_params=_cparams("parallel", "parallel"),
        name="mod_proj",
    )(x, sc, sh, w)


def _store_packed(o_ref, pairs):
    for q, (xa, xb) in enumerate(pairs):
        for h in range(xa.shape[1] // HEAD):
            sl = slice(HEAD * h, HEAD * (h + 1))
            o_ref[0, h, q] = jnp.concatenate([xa[:, sl], xb[:, sl]], axis=-1)


def _rwkv_prep_kernel(p_ref, pp_ref, pn_ref, sw_ref, w0_ref, a0_ref, kkp_ref, ka_ref, rk_ref,
                      w2h_ref, w2l_ref, a2h_ref, a2l_ref, g2h_ref, g2l_ref, ones_ref,
                      pkf_o, pkb_o, bonus_o, g_o, *, seq_len, width):
    tl = p_ref.shape[1]
    i = pl.program_id(1)
    pc = _conv3(p_ref[0], pp_ref[0, SUBLANES - 1:SUBLANES, :], pn_ref[0, 0:1, :],
                sw_ref[...], i * tl, seq_len)
    w = width
    r = pc[:, 0:w]
    k = pc[:, w:2 * w]
    v = pc[:, 2 * w:3 * w]
    wd = pc[:, 3 * w:3 * w + LANES]
    ad = pc[:, 3 * w + LANES:3 * w + 2 * LANES]
    gd = pc[:, 3 * w + 2 * LANES:3 * w + 3 * LANES]
    ones_bd = ones_ref[...]

    g_o[0] = _dot3(_sigmoid(gd), g2h_ref[...], g2l_ref[...])
    kkr = k * kkp_ref[...]
    kk = kkr * lax.rsqrt(_segsum(kkr * kkr, ones_bd) + 1e-12)
    lw = _dot3(jnp.tanh(wd), w2h_ref[...], w2l_ref[...])
    la = _dot3(ad, a2h_ref[...], a2l_ref[...])

    rrk = r * rk_ref[...]
    bonus = jnp.zeros_like(r)
    for d, pk_o in enumerate((pkf_o, pkb_o)):
        z = w0_ref[d:d + 1, :] + lw[:, d * w:(d + 1) * w]
        w_log = -(jnp.maximum(-z, 0.0) + jnp.log(1.0 + jnp.exp(-jnp.abs(z)))) - 0.5
        decay = jnp.exp(-jnp.exp(w_log))
        a = _sigmoid(a0_ref[d:d + 1, :] + la[:, d * w:(d + 1) * w])
        kd = k * (1.0 + (a - 1.0) * ka_ref[...])
        _store_packed(pk_o, ((r, v), (kk, decay), (kd, kk * a)))
        bonus = bonus + _segsum(rrk * kd, ones_bd) * v
    bonus_o[0] = bonus


def _rwkv_prep(p_r, lp, tl, seq_len):
    b, l, wp = p_r.shape
    w = lp["rwkv_w"]
    nh = w // HEAD
    hm = jax.ShapeDtypeStruct((b, nh, WKV_TILES, l, LANES), F32)
    tok = jax.ShapeDtypeStruct((b, l, w), F32)
    prev, nxt = _halo_specs(tl, wp, l)

    def full(a):
        return pl.BlockSpec(a.shape, lambda b_, i: (0,) * a.ndim)

    consts = [lp["rwkv_shift"], lp["rwkv_w0"], lp["rwkv_a0"], lp["rwkv_kk"], lp["rwkv_ka"],
              lp["rwkv_rk"], lp["w2_hi"], lp["w2_lo"], lp["a2_hi"], lp["a2_lo"],
              lp["g2_hi"], lp["g2_lo"], lp["ones_head"]]
    hm_spec = pl.BlockSpec((1, nh, WKV_TILES, tl, LANES), lambda b_, i: (b_, 0, 0, i, 0))
    tok_spec = pl.BlockSpec((1, tl, w), lambda b_, i: (b_, i, 0))
    return pl.pallas_call(
        functools.partial(_rwkv_prep_kernel, seq_len=seq_len, width=w),
        grid=(b, l // tl),
        in_specs=[pl.BlockSpec((1, tl, wp), lambda b_, i: (b_, i, 0)), prev, nxt]
        + [full(a) for a in consts],
        out_specs=[hm_spec] * 2 + [tok_spec] * 2,
        out_shape=[hm] * 2 + [tok] * 2,
        compiler_params=_cparams("parallel", "parallel"),
        name="rwkv_prep",
    )(p_r, p_r, p_r, *consts)


def _allsum8(x):
    s = x[0:8]
    for q in range(1, x.shape[0] // SUBLANES):
        s = s + x[q * SUBLANES:(q + 1) * SUBLANES]
    s = s + pltpu.roll(s, 4, 0)
    s = s + pltpu.roll(s, 2, 0)
    return s + pltpu.roll(s, 1, 0)


def _wkv_kernel(pf_ref, pb_ref, s0_ref, yf_ref, yb_ref, so_ref, S, OPS, YS, ACC, GAM):
    i = pl.program_id(1)
    n = pl.num_programs(1)
    t_steps = pf_ref.shape[3]
    stride = WKV_TILES * t_steps
    pf2 = pf_ref.reshape(HEAD * stride, LANES)
    pb2 = pb_ref.reshape(HEAD * stride, LANES)
    yf2 = yf_ref.reshape(HEAD * t_steps, LANES)
    yb2 = yb_ref.reshape(HEAD * t_steps, LANES)

    @pl.when(i == 0)
    def _():
        S[...] = jnp.concatenate([s0_ref[0], s0_ref[1]], axis=0).T

    V0, BT0, KDT0, KKT0, RT0 = (q * HEAD for q in range(5))
    BR0, KR0 = 5 * HEAD, 5 * HEAD + SUBLANES

    gam = jnp.ones((HEAD, LANES), F32)
    for t in range(t_steps):
        tiles = []
        for q in range(WKV_TILES):
            f = pf2[pl.ds(q * t_steps + t, HEAD, stride=stride), :]
            b = pb2[pl.ds(q * t_steps + (t_steps - 1 - t), HEAD, stride=stride), :]
            tiles.append(jnp.concatenate([f, b], axis=0).T)
        r, vv = tiles[0][:HEAD], tiles[0][HEAD:]
        kk, w = tiles[1][:HEAD], tiles[1][HEAD:]
        kd, b = tiles[2][:HEAD], tiles[2][HEAD:]
        OPS[t, KKT0:KKT0 + HEAD, :] = kk * gam
        gam = gam * w
        inv = 1.0 / gam
        OPS[t, V0:V0 + HEAD, :] = vv
        OPS[t, BT0:BT0 + HEAD, :] = b * inv
        OPS[t, KDT0:KDT0 + HEAD, :] = kd * inv
        OPS[t, RT0:RT0 + HEAD, :] = r * gam
        OPS[t, BR0:BR0 + SUBLANES, :] = _allsum8(b * r)
        OPS[t, KR0:KR0 + SUBLANES, :] = _allsum8(kd * r)
    GAM[...] = gam

    def accumulate(j, state_tile, k, sa, y0):
        return (sa + state_tile * OPS[j, KKT0 + k:KKT0 + k + 1, :],
                y0 + state_tile * OPS[j, RT0 + k:RT0 + k + 1, :])

    sa = jnp.zeros((HEAD, LANES), F32)
    y0 = jnp.zeros((HEAD, LANES), F32)
    for k in range(HEAD):
        sa, y0 = accumulate(0, S[k * HEAD:(k + 1) * HEAD, :], k, sa, y0)
    ACC[0:HEAD, :] = sa
    ACC[HEAD:2 * HEAD, :] = y0

    def step(j, carry):
        jn = jnp.minimum(j + 1, t_steps - 1)
        sa = ACC[0:HEAD, :]
        y0 = ACC[HEAD:2 * HEAD, :]
        vv = OPS[j, V0:V0 + HEAD, :]
        br = jnp.tile(OPS[j, BR0:BR0 + SUBLANES, :], (HEAD // SUBLANES, 1))
        kr = jnp.tile(OPS[j, KR0:KR0 + SUBLANES, :], (HEAD // SUBLANES, 1))
        YS[j] = y0 - sa * br + vv * kr
        san = jnp.zeros((HEAD, LANES), F32)
        y0n = jnp.zeros((HEAD, LANES), F32)
        for k in range(HEAD):
            rows = slice(k * HEAD, (k + 1) * HEAD)
            snew = (S[rows, :] - sa * OPS[j, BT0 + k:BT0 + k + 1, :]
                    + vv * OPS[j, KDT0 + k:KDT0 + k + 1, :])
            S[rows, :] = snew
            san, y0n = accumulate(jn, snew, k, san, y0n)
        ACC[0:HEAD, :] = san
        ACC[HEAD:2 * HEAD, :] = y0n
        return carry

    lax.fori_loop(0, t_steps, step, 0)

    for k in range(HEAD):
        rows = slice(k * HEAD, (k + 1) * HEAD)
        S[rows, :] = S[rows, :] * GAM[k:k + 1, :]

    zeros = jnp.zeros((HEAD, LANES), F32)
    for t in range(t_steps):
        yt = jnp.concatenate([YS[t], zeros], axis=0).T
        yf2[pl.ds(t, HEAD, stride=t_steps), :] = yt[:HEAD]
        yb2[pl.ds(t_steps - 1 - t, HEAD, stride=t_steps), :] = yt[HEAD:]

    @pl.when(i == n - 1)
    def _():
        st = S[...].T
        so_ref[0] = st[0:HEAD]
        so_ref[1] = st[HEAD:2 * HEAD]


def _wkv_scan(pkf, pkb, s0):
    b, nh, _, l, _ = pkf.shape
    gb = HEAD // nh
    assert gb * nh == HEAD and b % gb == 0 and l % SCAN_T == 0
    groups = b // gb
    n = l // SCAN_T
    pk_blk = (gb, nh, WKV_TILES, SCAN_T, LANES)
    y_blk = (gb, nh, SCAN_T, LANES)
    st = pl.BlockSpec((2, HEAD, HEAD * HEAD), lambda g, i: (0, g, 0))
    y_shape = jax.ShapeDtypeStruct((b, nh, l, LANES), F32)
    return pl.pallas_call(
        _wkv_kernel,
        grid=(groups, n),
        in_specs=[pl.BlockSpec(pk_blk, lambda g, i: (g, 0, 0, i, 0)),
                  pl.BlockSpec(pk_blk, lambda g, i: (g, 0, 0, n - 1 - i, 0)), st],
        out_specs=[pl.BlockSpec(y_blk, lambda g, i: (g, 0, i, 0)),
                   pl.BlockSpec(y_blk, lambda g, i: (g, 0, n - 1 - i, 0)), st],
        out_shape=[y_shape, y_shape, jax.ShapeDtypeStruct((2, b * nh, HEAD * HEAD), F32)],
        scratch_shapes=[pltpu.VMEM((HEAD * HEAD, LANES), F32),
                        pltpu.VMEM((SCAN_T, 5 * HEAD + 2 * SUBLANES, LANES), F32),
                        pltpu.VMEM((SCAN_T, HEAD, LANES), F32),
                        pltpu.VMEM((2 * HEAD, LANES), F32), pltpu.VMEM((HEAD, LANES), F32)],
        compiler_params=_cparams("parallel", "arbitrary"),
        name="wkv_scan",
    )(pkf, pkb, s0)


def _rwkv_post_kernel(yf_ref, yb_ref, bonus_ref, g_ref, gng_ref, gnb_ref, o_ref):
    y = yf_ref[0, :, :, 0:HEAD] + yb_ref[0, :, :, 0:HEAD]
    mu = jnp.mean(y, axis=-1, keepdims=True)
    yc = y - mu
    var = jnp.mean(yc * yc, axis=-1, keepdims=True)
    yn = yc * lax.rsqrt(var + GN_EPS)
    yt = jnp.concatenate([yn[h] for h in range(yn.shape[0])], axis=-1)
    o_ref[0] = (yt * gng_ref[...] + gnb_ref[...] + bonus_ref[0]) * g_ref[0]


def _rwkv_post(yf, yb, bonus, g, gn_g, gn_b, tl):
    b, nh, l, _ = yf.shape
    w = nh * HEAD
    hm = pl.BlockSpec((1, nh, tl, LANES), lambda b_, i: (b_, 0, i, 0))
    tok = pl.BlockSpec((1, tl, w), lambda b_, i: (b_, i, 0))
    vec = pl.BlockSpec((1, w), lambda b_, i: (0, 0))
    return pl.pallas_call(
        _rwkv_post_kernel,
        grid=(b, l // tl),
        in_specs=[hm, hm, tok, tok, vec, vec],
        out_specs=tok,
        out_shape=jax.ShapeDtypeStruct((b, l, w), F32),
        compiler_params=_cparams("parallel", "parallel"),
        name="rwkv_post",
    )(yf, yb, bonus, g, gn_g, gn_b)


def _hy_prep_kernel(p_ref, pp_ref, pn_ref, sw_ref, x0_o, u_o, ub_o, *, seq_len, width):
    tl = p_ref.shape[1]
    i = pl.program_id(1)
    pc = _conv3(p_ref[0], pp_ref[0, SUBLANES - 1:SUBLANES, :], pn_ref[0, 0:1, :],
                sw_ref[...], i * tl, seq_len)
    w = width
    x0_o[0] = pc[:, 0:w]
    u = pc[:, w:2 * w] * pc[:, 2 * w:3 * w]
    u_o[0] = u
    ub_o[0] = u.astype(BF16)


def _hy_prep(p_h, short_w, tl, seq_len):
    b, l, w3 = p_h.shape
    w = w3 // 3
    prev, nxt = _halo_specs(tl, w3, l)
    tok = pl.BlockSpec((1, tl, w), lambda b_, i: (b_, i, 0))
    return pl.pallas_call(
        functools.partial(_hy_prep_kernel, seq_len=seq_len, width=w),
        grid=(b, l // tl),
        in_specs=[pl.BlockSpec((1, tl, w3), lambda b_, i: (b_, i, 0)), prev, nxt,
                  pl.BlockSpec((3, w3), lambda b_, i: (0, 0))],
        out_specs=[tok, tok, tok],
        out_shape=[jax.ShapeDtypeStruct((b, l, w), F32), jax.ShapeDtypeStruct((b, l, w), F32),
                   jax.ShapeDtypeStruct((b, l, w), BF16)],
        compiler_params=_cparams("parallel", "parallel"),
        name="hy_prep",
    )(p_h, p_h, p_h, short_w)


def _hy_mlp_kernel(z_ref, w1_ref, b1_ref, f_ref, w2_ref, b2_ref, w3_ref, dec_ref, h_o):
    z = z_ref[...]
    t01 = z[:, 0:1]
    w1h, w1l = _split2(w1_ref[...])
    w2h, w2l = _split2(w2_ref[...])
    w3h, w3l = _split2(w3_ref[...])
    h = jnp.sin(f_ref[0:1, :] * (_dot3(z, w1h, w1l) + b1_ref[...]))
    h = jnp.sin(f_ref[1:2, :] * (_dot3(h, w2h, w2l) + b2_ref[...]))
    h = _dot3(h, w3h, w3l) * jnp.exp(-t01 * jnp.abs(dec_ref[...]))
    h_o[0] = h.astype(BF16)


def _hy_mlp(z, lp, tl):
    n = z.shape[0]
    c2 = lp["hy_w3"].shape[1]

    def full(a):
        return pl.BlockSpec(a.shape, lambda i: (0,) * a.ndim)

    consts = [lp["hy_w1"], lp["hy_b1"], lp["hy_freq"], lp["hy_w2"], lp["hy_b2"], lp["hy_w3"],
              lp["hy_decay"]]
    out = pl.BlockSpec((1, tl, c2), lambda i: (0, i, 0))
    return pl.pallas_call(
        _hy_mlp_kernel,
        grid=(n // tl,),
        in_specs=[pl.BlockSpec((tl, LANES), lambda i: (i, 0))] + [full(a) for a in consts],
        out_specs=out,
        out_shape=jax.ShapeDtypeStruct((1, n, c2), BF16),
        compiler_params=_cparams("parallel"),
        name="hy_mlp",
    )(z, *consts)


def _dft_filter_kernel(fr_ref, fi_ref, x_ref, hre_o, him_o):
    x = x_ref[0]
    w = x.shape[1] // 2
    xre = _dot(fr_ref[...], x)
    xim = _dot(fi_ref[...], x)
    hre_o[...] = xre[:, :w] + xre[:, w:]
    row = lax.broadcasted_iota(jnp.int32, (xre.shape[0], 1), 0) + pl.program_id(0) * xre.shape[0]
    him_o[...] = jnp.where(row == 0, xim[:, :w] + xim[:, w:], xim[:, :w] - xim[:, w:])


def _dft_filter(mats, h, tf):
    n = h.shape[1]
    w = h.shape[2] // 2
    ft = pl.BlockSpec((tf, n), lambda i: (i, 0))
    xs = pl.BlockSpec((1, n, 2 * w), lambda i: (0, 0, 0))
    out = pl.BlockSpec((tf, w), lambda i: (i, 0))
    return pl.pallas_call(
        _dft_filter_kernel,
        grid=(n // tf,),
        in_specs=[ft, ft, xs],
        out_specs=[out, out],
        out_shape=[jax.ShapeDtypeStruct((n, w), F32)] * 2,
        compiler_params=_cparams("parallel"),
        name="dft_filter",
    )(mats["fre"], mats["fim"], h)


def _dft_signal_kernel(fr_ref, fi_ref, x_ref, hre_ref, him_ref, pr_o, pi_o):
    x = x_ref[0]
    xre = _dot(fr_ref[...], x)
    xim = _dot(fi_ref[...], x)
    hre, him = hre_ref[...], him_ref[...]
    row = lax.broadcasted_iota(jnp.int32, (xre.shape[0], 1), 0) + pl.program_id(1) * xre.shape[0]
    pr_o[0] = jnp.where(row == 0, xre * hre, xre * hre - xim * him).astype(BF16)
    pi_o[0] = jnp.where(row == 0, xim * him, xre * him + xim * hre).astype(BF16)


def _dft_signal(mats, ub, hre, him, tf):
    b, n, w = ub.shape
    ft = pl.BlockSpec((tf, n), lambda b_, i: (i, 0))
    xs = pl.BlockSpec((1, n, w), lambda b_, i: (b_, 0, 0))
    hs = pl.BlockSpec((tf, w), lambda b_, i: (i, 0))
    out = pl.BlockSpec((1, tf, w), lambda b_, i: (b_, i, 0))
    return pl.pallas_call(
        _dft_signal_kernel,
        grid=(b, n // tf),
        in_specs=[ft, ft, xs, hs, hs],
        out_specs=[out] * 2,
        out_shape=[jax.ShapeDtypeStruct((b, n, w), BF16)] * 2,
        compiler_params=_cparams("parallel", "parallel"),
        name="dft_signal",
    )(mats["fre"], mats["fim"], ub, hre, him)


def _dft_inverse_kernel(gr_ref, gi_ref, pr_ref, pi_ref, x0_ref, u_ref, bias_ref, o_ref):
    y = _dot(gr_ref[...], pr_ref[0]) + _dot(gi_ref[...], pi_ref[0])
    o_ref[0] = x0_ref[0] * (y + u_ref[0] * bias_ref[...])


def _dft_inverse(mats, pr, pi, x0, u, bias, tt):
    b, n, w = u.shape
    gt = pl.BlockSpec((tt, n), lambda b_, i: (i, 0))
    ps = pl.BlockSpec((1, n, w), lambda b_, i: (b_, 0, 0))
    tok = pl.BlockSpec((1, tt, w), lambda b_, i: (b_, i, 0))
    return pl.pallas_call(
        _dft_inverse_kernel,
        grid=(b, n // tt),
        in_specs=[gt, gt, ps, ps, tok, tok, pl.BlockSpec((1, w), lambda b_, i: (0, 0))],
        out_specs=tok,
        out_shape=jax.ShapeDtypeStruct((b, n, w), F32),
        compiler_params=_cparams("parallel", "parallel"),
        name="dft_inverse",
    )(mats["gre"], mats["gim"], pr, pi, x0, u, bias)


def _swap_halves(x):
    lane = lax.broadcasted_iota(jnp.int32, x.shape, 1)
    half = HEAD // 2
    return jnp.where(lane % HEAD < half, pltpu.roll(x, LANES - half, 1), pltpu.roll(x, half, 1))


def _attn_prep_kernel(*refs, rope, wq, wkv):
    if rope:
        p_ref, qn_ref, kn_ref, ones_ref, cos_ref, sin_ref, q_o, k_o, v_o, kt_o = refs
    else:
        p_ref, qn_ref, kn_ref, ones_ref, q_o, k_o, v_o, kt_o = refs
    ones_bd = ones_ref[...]

    def norm_rope(x, gain):
        ms = _segsum(x * x, ones_bd) * (1.0 / HEAD)
        xn = x * lax.rsqrt(ms + QK_EPS) * gain
        if rope:
            xn = xn * cos_ref[...] + _swap_halves(xn) * sin_ref[...]
        return xn

    scale = HEAD ** -0.5
    for c in range(wq // LANES):
        xq = norm_rope(p_ref[0, :, c * LANES:(c + 1) * LANES], qn_ref[...]) * scale
        q_o[0, 2 * c] = xq[:, :HEAD].astype(BF16)
        q_o[0, 2 * c + 1] = xq[:, HEAD:].astype(BF16)
    for c in range(wkv // LANES):
        x = p_ref[0, :, wq + c * LANES:wq + (c + 1) * LANES]
        ms = _segsum(x * x, ones_bd) * (1.0 / HEAD)
        xk = x * lax.rsqrt(ms + QK_EPS) * kn_ref[...]
        kt_o[0, :, c * LANES:(c + 1) * LANES] = xk
        if rope:
            xk = xk * cos_ref[...] + _swap_halves(xk) * sin_ref[...]
        k_o[0, 2 * c] = xk[:, :HEAD].astype(BF16)
        k_o[0, 2 * c + 1] = xk[:, HEAD:].astype(BF16)
        xv = p_ref[0, :, wq + wkv + c * LANES:wq + wkv + (c + 1) * LANES]
        v_o[0, 2 * c] = xv[:, :HEAD].astype(BF16)
        v_o[0, 2 * c + 1] = xv[:, HEAD:].astype(BF16)


def _attn_prep(p_a, lp, rope_tabs, tl, wq, wkv):
    b, l, wa = p_a.shape
    nq, nkv = wq // HEAD, wkv // HEAD
    rope = rope_tabs is not None
    vec = pl.BlockSpec((1, LANES), lambda b_, i: (0, 0))
    in_specs = [pl.BlockSpec((1, tl, wa), lambda b_, i: (b_, i, 0)), vec, vec,
                pl.BlockSpec((LANES, LANES), lambda b_, i: (0, 0))]
    args = [p_a, lp["attn_qn"], lp["attn_kn"], lp["ones_pair"]]
    if rope:
        tab = pl.BlockSpec((tl, LANES), lambda b_, i: (i, 0))
        in_specs += [tab, tab]
        args += list(rope_tabs)
    return pl.pallas_call(
        functools.partial(_attn_prep_kernel, rope=rope, wq=wq, wkv=wkv),
        grid=(b, l // tl),
        in_specs=in_specs,
        out_specs=[pl.BlockSpec((1, nq, tl, HEAD), lambda b_, i: (b_, 0, i, 0)),
                   pl.BlockSpec((1, nkv, tl, HEAD), lambda b_, i: (b_, 0, i, 0)),
                   pl.BlockSpec((1, nkv, tl, HEAD), lambda b_, i: (b_, 0, i, 0)),
                   pl.BlockSpec((1, tl, wkv), lambda b_, i: (b_, i, 0))],
        out_shape=[jax.ShapeDtypeStruct((b, nq, l, HEAD), BF16),
                   jax.ShapeDtypeStruct((b, nkv, l, HEAD), BF16),
                   jax.ShapeDtypeStruct((b, nkv, l, HEAD), BF16),
                   jax.ShapeDtypeStruct((b, l, wkv), F32)],
        compiler_params=_cparams("parallel", "parallel"),
        name="attn_prep",
    )(*args)


def _attn_kernel(q_ref, k_ref, v_ref, o_ref):
    k, v = k_ref[0, 0], v_ref[0, 0]
    outs = []
    for h in range(q_ref.shape[1]):
        s = lax.dot_general(q_ref[0, h], k, (((1,), (1,)), ((), ())), preferred_element_type=F32)
        m = jnp.max(s, axis=-1, keepdims=True)
        p = jnp.exp(s - m)
        den = jnp.sum(p, axis=-1, keepdims=True)
        outs.append(_dot(p.astype(BF16), v) / den)
    o_ref[0] = jnp.concatenate(outs, axis=-1)


def _attention(q, k, v, tq):
    b, nq, l, _ = q.shape
    nkv, lk = k.shape[1], k.shape[2]
    g = nq // nkv
    kv = pl.BlockSpec((1, 1, lk, HEAD), lambda b_, h, i: (b_, h, 0, 0))
    return pl.pallas_call(
        _attn_kernel,
        grid=(b, nkv, l // tq),
        in_specs=[pl.BlockSpec((1, g, tq, HEAD), lambda b_, h, i: (b_, h, i, 0)), kv, kv],
        out_specs=pl.BlockSpec((1, tq, g * HEAD), lambda b_, h, i: (b_, i, h)),
        out_shape=jax.ShapeDtypeStruct((b, l, nq * HEAD), F32),
        compiler_params=_cparams("parallel", "parallel", "parallel"),
        name="attention",
    )(q, k, v)


def _out_proj_kernel(yr_ref, yh_ref, ya_ref, x_ref, g_ref, w_ref, lng_ref, lnb_ref, o_ref, *, alpha):
    wr, wh = yr_ref.shape[2], yh_ref.shape[2]
    mix = _dot(yr_ref[0].astype(BF16), w_ref[0:wr, :])
    mix = mix + _dot(yh_ref[0].astype(BF16), w_ref[wr:wr + wh, :])
    mix = mix + _dot(ya_ref[0].astype(BF16), w_ref[wr + wh:, :])
    o_ref[0] = _layer_norm(alpha * x_ref[0] + g_ref[0] * mix, lng_ref[...], lnb_ref[...])


def _out_proj(y_r, y_h, y_a, x, gate, w_out, ln_g, ln_b, tl, alpha):
    bx, lx, d = x.shape
    sel = _bm(gate)

    def tok(a):
        return pl.BlockSpec((1, tl, a.shape[2]), lambda b, i: (b, i, 0))

    vec = pl.BlockSpec((1, d), lambda b, i: (0, 0))
    return pl.pallas_call(
        functools.partial(_out_proj_kernel, alpha=alpha),
        grid=(bx, lx // tl),
        in_specs=[tok(y_r), tok(y_h), tok(y_a), tok(x),
                  pl.BlockSpec((1, 1, d), lambda b, i: (sel(b), 0, 0)),
                  pl.BlockSpec(w_out.shape, lambda b, i: (0, 0)), vec, vec],
        out_specs=tok(x),
        out_shape=jax.ShapeDtypeStruct((bx, lx, d), F32),
        compiler_params=_cparams("parallel", "parallel"),
        name="out_proj",
    )(y_r, y_h, y_a, x, gate, w_out, ln_g, ln_b)


def _ffn_kernel(x_ref, xp_ref, xn_ref, sc_ref, sh_ref, g_ref, wa_ref, wb_ref, ca_ref, cb_ref,
                wd_ref, lng_ref, lnb_ref, o_ref, h_s, acc_s, *, seq_len, alpha):
    tl = x_ref.shape[1]
    i = pl.program_id(1)
    j = pl.program_id(2)
    halo = FFN_HALO

    @pl.when(j == 0)
    def _():
        sc, sh = 1.0 + sc_ref[0], sh_ref[0]
        h_s[0:halo] = (xp_ref[0] * sc + sh).astype(BF16)
        h_s[halo:halo + tl] = (x_ref[0] * sc + sh).astype(BF16)
        h_s[halo + tl:] = (xn_ref[0] * sc + sh).astype(BF16)
        acc_s[...] = jnp.zeros_like(acc_s)

    row = lax.broadcasted_iota(jnp.int32, (tl, 1), 0)
    pos = (i * tl + row) % seq_len
    first = pos == 0
    last = pos == seq_len - 1
    h = h_s[...]

    def conv_up(w_ref, c_ref):
        u = _dot(h, w_ref[...])
        um = jnp.where(first, 0.0, pltpu.roll(u, 1, 0)[halo:halo + tl])
        up = jnp.where(last, 0.0, pltpu.roll(u, tl + 2 * halo - 1, 0)[halo:halo + tl])
        return um * c_ref[0:1, :] + u[halo:halo + tl] * c_ref[1:2, :] + up * c_ref[2:3, :]

    a = conv_up(wa_ref, ca_ref)
    b = conv_up(wb_ref, cb_ref)
    f = (a * _sigmoid(a) * b).astype(BF16)
    acc_s[...] += _dot(f, wd_ref[...])

    @pl.when(j == pl.num_programs(2) - 1)
    def _():
        o_ref[0] = _layer_norm(alpha * x_ref[0] + g_ref[0] * acc_s[...], lng_ref[...], lnb_ref[...])


def _ffn(x, sc, sh, gate, w_up, conv_w, w_down, ln_g, ln_b, tl, tn, seq_len, alpha):
    bx, lx, d = x.shape
    dff = w_down.shape[0]
    nj = dff // tn
    sel = _bm(sc)
    prev, nxt = _halo_specs(tl, d, lx, FFN_HALO)
    prev3 = pl.BlockSpec(prev.block_shape, lambda b, i, j: prev.index_map(b, i))
    nxt3 = pl.BlockSpec(nxt.block_shape, lambda b, i, j: nxt.index_map(b, i))
    mod = pl.BlockSpec((1, 1, d), lambda b, i, j: (sel(b), 0, 0))
    vec = pl.BlockSpec((1, d), lambda b, i, j: (0, 0))
    tok = pl.BlockSpec((1, tl, d), lambda b, i, j: (b, i, 0))
    return pl.pallas_call(
        functools.partial(_ffn_kernel, seq_len=seq_len, alpha=alpha),
        grid=(bx, lx // tl, nj),
        in_specs=[tok, prev3, nxt3, mod, mod, mod,
                  pl.BlockSpec((d, tn), lambda b, i, j: (0, j)),
                  pl.BlockSpec((d, tn), lambda b, i, j: (0, j + nj)),
                  pl.BlockSpec((3, tn), lambda b, i, j: (0, j)),
                  pl.BlockSpec((3, tn), lambda b, i, j: (0, j + nj)),
                  pl.BlockSpec((tn, d), lambda b, i, j: (j, 0)), vec, vec],
        out_specs=tok,
        out_shape=jax.ShapeDtypeStruct((bx, lx, d), F32),
        scratch_shapes=[pltpu.VMEM((tl + 2 * FFN_HALO, d), BF16), pltpu.VMEM((tl, d), F32)],
        compiler_params=_cparams("parallel", "parallel", "arbitrary"),
        name="conv_ffn",
    )(x, x, x, sc, sh, gate, w_up, w_up, conv_w, conv_w, w_down, ln_g, ln_b)


def _split2_host(x):
    hi = lax.bitcast_convert_type(lax.bitcast_convert_type(x, jnp.uint32) & jnp.uint32(0xFFFF0000), F32)
    return hi.astype(BF16), (x - hi).astype(BF16)


def _dft_mats(n):
    big = 2 * n
    lo = min(n, 64)
    k = jnp.arange(n, dtype=jnp.int32)[:, None]

    def table(t):
        ang = ((k * t[None, :]) % big).astype(F32) * (2.0 * math.pi / big)
        return jnp.cos(ang), jnp.sin(ang)

    ca, sa = table(lo * jnp.arange(n // lo, dtype=jnp.int32))
    cb, sb = table(jnp.arange(lo, dtype=jnp.int32))
    cos = (ca[:, :, None] * cb[:, None, :] - sa[:, :, None] * sb[:, None, :]).reshape(n, n)
    sin = (sa[:, :, None] * cb[:, None, :] + ca[:, :, None] * sb[:, None, :]).reshape(n, n)
    t = jnp.arange(n, dtype=jnp.int32)[None, :]
    fre = cos
    fim = jnp.where(k == 0, jnp.where(t % 2 == 0, 1.0, -1.0), -sin)
    scale = jnp.where(k == 0, 1.0 / big, 2.0 / big)
    return {"fre": fre.astype(BF16), "fim": fim.astype(BF16),
            "gre": (fre * scale).T.astype(BF16), "gim": (fim * scale).T.astype(BF16)}


def _hyena_features(n, n_bands):
    t01 = jnp.linspace(0.0, 1.0, n, dtype=F32)[:, None]
    pos = jnp.arange(n, dtype=F32)[:, None]
    bands = jnp.linspace(1e-4, n_bands - 1, n_bands, dtype=F32)[None, :]
    ang = (2.0 * math.pi / n) * pos * bands
    z = jnp.concatenate([t01, jnp.cos(ang), -jnp.sin(ang)], -1)
    return jnp.pad(z, ((0, 0), (0, LANES - z.shape[1])))


def _rope_tables(n_tokens):
    rows = n_tokens // GRID_W
    row = jnp.repeat(jnp.arange(rows, dtype=F32), GRID_W)
    col = jnp.tile(jnp.arange(GRID_W, dtype=F32), rows)
    n_freq = HEAD // 4
    inv = ROPE_THETA ** (-jnp.arange(n_freq, dtype=F32) / n_freq)
    ang = jnp.concatenate([row[:, None] * inv, col[:, None] * inv], -1)
    cos, sin = jnp.cos(ang), jnp.sin(ang)
    cos2 = jnp.tile(jnp.concatenate([cos, cos], -1), (1, LANES // HEAD))
    sin2 = jnp.tile(jnp.concatenate([-sin, sin], -1), (1, LANES // HEAD))
    return cos2, sin2


def _block_ones(n, group):
    idx = jnp.arange(n) // group
    return (idx[:, None] == idx[None, :]).astype(BF16)


def _pad_to(a, shape):
    return jnp.pad(a, [(0, s - d) for d, s in zip(a.shape, shape)])


def _block_diag2(a, b):
    za = jnp.zeros((a.shape[0], b.shape[1]), a.dtype)
    zb = jnp.zeros((b.shape[0], a.shape[1]), a.dtype)
    return jnp.concatenate([jnp.concatenate([a, za], 1), jnp.concatenate([zb, b], 1)], 0)


def _layer_params(l, P, dims):
    w_r, w_h, wq, wkv = dims["w_rwkv"], dims["w_hyena"], dims["wq"], dims["wkv"]
    rwkv_cols = 3 * w_r + 2 * P["rwkv_w2"].shape[2] + 2 * P["rwkv_a2"].shape[2] + P["rwkv_g2"].shape[1]
    rwkv_pad = -(-rwkv_cols // (2 * LANES)) * (2 * LANES)
    hy_cols = 3 * w_h
    w_in = P["w_in"][l]
    lp = {"rwkv_w": w_r}
    lp["w_in_r"] = _pad_to(w_in[:, :rwkv_cols], (w_in.shape[0], rwkv_pad)).astype(BF16)
    lp["w_in_h"] = w_in[:, rwkv_cols:rwkv_cols + hy_cols].astype(BF16)
    lp["w_in_a"] = w_in[:, rwkv_cols + hy_cols:].astype(BF16)
    lp["rwkv_shift"] = _pad_to(P["rwkv_shift"][l], (3, rwkv_pad))
    lp["rwkv_w0"] = P["rwkv_w0"][l]
    lp["rwkv_a0"] = P["rwkv_a0"][l]
    for nm in ("rwkv_kk", "rwkv_ka", "rwkv_rk", "rwkv_gn_g", "rwkv_gn_b", "hy_bias",
               "ln1_g", "ln1_b", "ln2_g", "ln2_b"):
        lp[nm] = P[nm][l][None, :]
    lp["w2_hi"], lp["w2_lo"] = _split2_host(_block_diag2(P["rwkv_w2"][l, 0], P["rwkv_w2"][l, 1]))
    lp["a2_hi"], lp["a2_lo"] = _split2_host(_block_diag2(P["rwkv_a2"][l, 0], P["rwkv_a2"][l, 1]))
    lp["g2_hi"], lp["g2_lo"] = _split2_host(P["rwkv_g2"][l])
    lp["ones_head"] = _block_ones(w_r, HEAD)
    lp["ones_pair"] = _block_ones(LANES, HEAD)
    lp["hy_short"] = P["hy_short"][l]
    ffn_w = P["hy_w1"].shape[2]
    lp["hy_w1"] = _pad_to(P["hy_w1"][l], (LANES, LANES))
    lp["hy_b1"] = _pad_to(P["hy_b1"][l][None, :], (1, LANES))
    lp["hy_freq"] = _pad_to(P["hy_freq"][l], (2, LANES))
    lp["hy_w2"] = _pad_to(P["hy_w2"][l], (LANES, LANES))
    lp["hy_b2"] = _pad_to(P["hy_b2"][l][None, :], (1, LANES))
    lp["hy_w3"] = _pad_to(P["hy_w3"][l], (LANES, 2 * w_h))
    lp["hy_decay"] = P["hy_decay"][l].reshape(1, 2 * w_h)
    del ffn_w
    lp["attn_qn"] = jnp.tile(P["attn_qn"][l], LANES // HEAD)[None, :]
    lp["attn_kn"] = jnp.tile(P["attn_kn"][l], LANES // HEAD)[None, :]
    lp["w_out"] = P["w_out"][l].astype(BF16)
    lp["ffn_up"] = P["ffn_up"][l].astype(BF16)
    lp["ffn_conv"] = P["ffn_conv"][l]
    lp["ffn_down"] = P["ffn_down"][l].astype(BF16)
    return lp


def _trunk_layer(x, mod6, lp, dims, rope_tabs, ctx_kv, s0, dft, z_feat, tiles):
    b, l, d = x.shape
    sh1, sc1, g1, sh2, sc2, g2 = mod6
    shared = sh1.shape[0] == 1
    alpha = dims["alpha"]
    w_r, w_h, wq, wkv = dims["w_rwkv"], dims["w_hyena"], dims["wq"], dims["wkv"]
    nh = w_r // HEAD
    tl_mm, tl_ew = tiles["mm"], tiles["ew"]

    xm = x.reshape(1, b * l, d) if shared else x
    p_r = _mod_proj(xm, sc1, sh1, lp["w_in_r"], tl_mm).reshape(b, l, -1)
    p_h = _mod_proj(xm, sc1, sh1, lp["w_in_h"], tl_mm).reshape(b, l, -1)
    p_a = _mod_proj(xm, sc1, sh1, lp["w_in_a"], tl_mm).reshape(b, l, -1)

    pkf, pkb, bonus, gate = _rwkv_prep(p_r, lp, tl_ew, l)
    yf, yb, s_fin = _wkv_scan(pkf, pkb, s0)
    y_r = _rwkv_post(yf, yb, bonus, gate, lp["rwkv_gn_g"], lp["rwkv_gn_b"], tl_ew)

    x0, u, ub = _hy_prep(p_h, lp["hy_short"], tl_ew, l)
    tf = min(l, 256)
    hre, him = _dft_filter(dft, _hy_mlp(z_feat, lp, tf), tf)
    pr, pi = _dft_signal(dft, ub, hre, him, tf)
    y_h = _dft_inverse(dft, pr, pi, x0, u, lp["hy_bias"], tf)

    q, k, vv, k_tok = _attn_prep(p_a, lp, rope_tabs, tl_ew, wq, wkv)
    if ctx_kv is not None:
        ck, cv = ctx_kv
        k = jnp.concatenate([k, jnp.swapaxes(ck, 1, 2).astype(BF16)], axis=2)
        vv = jnp.concatenate([vv, jnp.swapaxes(cv, 1, 2).astype(BF16)], axis=2)
    y_a = _attention(q, k, vv, tiles["tq"])

    def m(a):
        return a.reshape(1, b * l, a.shape[-1]) if shared else a

    x1 = _out_proj(m(y_r), m(y_h), m(y_a), xm, g1, lp["w_out"], lp["ln1_g"], lp["ln1_b"],
                   tl_mm, alpha)
    x2 = _ffn(x1, sc2, sh2, g2, lp["ffn_up"], lp["ffn_conv"], lp["ffn_down"],
              lp["ln2_g"], lp["ln2_b"], tl_mm, tiles["ffn_tn"], l, alpha)
    v_tok = p_a[..., wq + wkv:]
    return x2.reshape(b, l, d), k_tok, v_tok, s_fin


def kernel(x_prompt, x_sample, cache_k, cache_v, state_rwkv, c, c_ctx, w_mod, b_mod, w_in, rwkv_shift, rwkv_w0, rwkv_w2, rwkv_a0, rwkv_a2, rwkv_kk, rwkv_ka, rwkv_rk, rwkv_g2, rwkv_gn_g, rwkv_gn_b, hy_short, hy_w1, hy_b1, hy_freq, hy_w2, hy_b2, hy_w3, hy_decay, hy_bias, attn_qn, attn_kn, w_out, ln1_g, ln1_b, ln2_g, ln2_b, ffn_up, ffn_conv, ffn_down):
    P = dict(w_in=w_in, rwkv_shift=rwkv_shift, rwkv_w0=rwkv_w0, rwkv_w2=rwkv_w2, rwkv_a0=rwkv_a0,
             rwkv_a2=rwkv_a2, rwkv_kk=rwkv_kk, rwkv_ka=rwkv_ka, rwkv_rk=rwkv_rk, rwkv_g2=rwkv_g2,
             rwkv_gn_g=rwkv_gn_g, rwkv_gn_b=rwkv_gn_b, hy_short=hy_short, hy_w1=hy_w1, hy_b1=hy_b1,
             hy_freq=hy_freq, hy_w2=hy_w2, hy_b2=hy_b2, hy_w3=hy_w3, hy_decay=hy_decay,
             hy_bias=hy_bias, attn_qn=attn_qn, attn_kn=attn_kn, w_out=w_out, ln1_g=ln1_g,
             ln1_b=ln1_b, ln2_g=ln2_g, ln2_b=ln2_b, ffn_up=ffn_up, ffn_conv=ffn_conv,
             ffn_down=ffn_down)
    depth, d = w_mod.shape[0], w_mod.shape[1]
    bc, lc, _ = x_prompt.shape
    bd, ld, _ = x_sample.shape
    nkv = cache_k.shape[3]
    nh = state_rwkv.shape[3]
    w_r = nh * HEAD
    w_h = hy_bias.shape[1]
    wkv = nkv * HEAD
    wq = w_in.shape[2] - (3 * w_r + 2 * rwkv_w2.shape[2] + 2 * rwkv_a2.shape[2] + rwkv_g2.shape[1]) \
        - 3 * w_h - 2 * wkv
    dims = dict(w_rwkv=w_r, w_hyena=w_h, wq=wq, wkv=wkv, alpha=(2 * depth) ** 0.25)

    rows = -(-(bd + 1) // SUBLANES) * SUBLANES
    cond = _pad_to(jnp.concatenate([c, c_ctx[None, :]], 0), (rows, d))
    mod = _modulation(cond, w_mod, b_mod)

    rope_tabs = _rope_tables(ld)
    dft_c, dft_d = _dft_mats(lc), _dft_mats(ld)
    n_bands = (hy_w1.shape[1] - 1) // 2
    z_c, z_d = _hyena_features(lc, n_bands), _hyena_features(ld, n_bands)
    tiles_c = dict(mm=min(512, bc * lc), ew=min(256, lc), tq=min(256, lc), ffn_tn=512)
    tiles_d = dict(mm=min(512, ld), ew=min(256, ld), tq=min(512, ld), ffn_tn=512)

    xp, xs = x_prompt, x_sample
    zero_state = jnp.zeros((2, bc * nh, HEAD * HEAD), F32)
    new_k, new_v, new_s = [], [], []
    for l in range(depth):
        lp = _layer_params(l, P, dims)
        mod_d = [m[:bd, None, :] for m in jnp.split(mod[l], 6, axis=-1)]
        mod_c = [m[bd:bd + 1, None, :] for m in jnp.split(mod[l], 6, axis=-1)]
        xp, k_c, v_c, s_c = _trunk_layer(xp, mod_c, lp, dims, None, None, zero_state, dft_c, z_c, tiles_c)
        new_k.append(k_c.reshape(bc, lc, nkv, HEAD))
        new_v.append(v_c.reshape(bc, lc, nkv, HEAD))
        new_s.append(jnp.transpose(s_c.reshape(2, bc, nh, HEAD, HEAD), (1, 0, 2, 4, 3)))
        s0 = jnp.transpose(state_rwkv[:, l], (1, 0, 2, 4, 3)).reshape(2, bd * nh, HEAD * HEAD)
        xs, _, _, _ = _trunk_layer(xs, mod_d, lp, dims, rope_tabs, (cache_k[:, l], cache_v[:, l]),
                                   s0, dft_d, z_d, tiles_d)
    return (xp, xs, jnp.stack(new_k, axis=1), jnp.stack(new_v, axis=1), jnp.stack(new_s, axis=1))
```

```python
import functools
import math

import jax
import jax.numpy as jnp
from jax import lax
from jax.experimental import pallas as pl
from jax.experimental.pallas import tpu as pltpu

F32 = jnp.float32
BF16 = jnp.bfloat16

HEAD = 64
LANES = 128
SUBLANES = 8
VMEM_LIMIT = 52 * 1024 * 1024
LN_EPS = 1e-5
QK_EPS = 1e-6
GN_EPS = 64e-5
ROPE_THETA = 10000.0
GRID_W = 64
SCAN_T = 16
WKV_TILES = 3
FFN_HALO = 16


def _cparams(*sem):
    return pltpu.CompilerParams(dimension_semantics=sem, vmem_limit_bytes=VMEM_LIMIT)


def _dot(a, b):
    return jnp.dot(a, b, preferred_element_type=F32)


def _hi_f32(x):
    u = pltpu.bitcast(x, jnp.uint32) & jnp.uint32(0xFFFF0000)
    return pltpu.bitcast(u, F32)


def _split2(x):
    h = _hi_f32(x)
    return h.astype(BF16), (x - h).astype(BF16)


def _split3(x):
    h1 = _hi_f32(x)
    r1 = x - h1
    h2 = _hi_f32(r1)
    return h1.astype(BF16), h2.astype(BF16), (r1 - h2).astype(BF16)


def _dot3(a, b_hi, b_lo):
    a_hi, a_lo = _split2(a)
    return _dot(a_hi, b_hi) + (_dot(a_lo, b_hi) + _dot(a_hi, b_lo))


def _segsum(x, ones_bd):
    h1, h2, h3 = _split3(x)
    return _dot(h1, ones_bd) + (_dot(h2, ones_bd) + _dot(h3, ones_bd))


def _layer_norm(x, g, b):
    mu = jnp.mean(x, axis=-1, keepdims=True)
    xc = x - mu
    var = jnp.mean(xc * xc, axis=-1, keepdims=True)
    return xc * lax.rsqrt(var + LN_EPS) * g + b


def _sigmoid(x):
    return 1.0 / (1.0 + jnp.exp(-x))


def _conv3(x, prev_row, next_row, w, row0, seq_len):
    tl = x.shape[0]
    row = lax.broadcasted_iota(jnp.int32, (tl, 1), 0)
    pos = (row0 + row) % seq_len
    xm = jnp.where(row == 0, prev_row, pltpu.roll(x, 1, 0))
    xp = jnp.where(row == tl - 1, next_row, pltpu.roll(x, tl - 1, 0))
    xm = jnp.where(pos == 0, 0.0, xm)
    xp = jnp.where(pos == seq_len - 1, 0.0, xp)
    return xm * w[0:1, :] + x * w[1:2, :] + xp * w[2:3, :]


def _halo_specs(tl, width, n_rows, halo=SUBLANES):
    r = tl // halo
    last = n_rows // halo - 1
    prev = pl.BlockSpec((1, halo, width), lambda b, i: (b, jnp.maximum(i * r - 1, 0), 0))
    nxt = pl.BlockSpec((1, halo, width), lambda b, i: (b, jnp.minimum((i + 1) * r, last), 0))
    return prev, nxt


def _bm(arr):
    if arr.shape[0] == 1:
        return lambda b: 0
    return lambda b: b


def _mod_kernel(c_ref, w_ref, b_ref, o_ref):
    c = c_ref[...]
    s = c * _sigmoid(c)
    w_hi, w_lo = _split2(w_ref[0])
    o_ref[0] = _dot3(s, w_hi, w_lo) + b_ref[0]


def _modulation(cond, w_mod, b_mod):
    depth, d, n = w_mod.shape
    rows = cond.shape[0]
    tn = 1024
    return pl.pallas_call(
        _mod_kernel,
        grid=(depth, n // tn),
        in_specs=[
            pl.BlockSpec((rows, d), lambda l, j: (0, 0)),
            pl.BlockSpec((1, d, tn), lambda l, j: (l, 0, j)),
            pl.BlockSpec((1, 1, tn), lambda l, j: (l, 0, j)),
        ],
        out_specs=pl.BlockSpec((1, rows, tn), lambda l, j: (l, 0, j)),
        out_shape=jax.ShapeDtypeStruct((depth, rows, n), F32),
        compiler_params=_cparams("parallel", "parallel"),
        name="modulation",
    )(cond, w_mod, b_mod.reshape(depth, 1, n))


def _proj_kernel(x_ref, sc_ref, sh_ref, w_ref, o_ref):
    h = (x_ref[0] * (1.0 + sc_ref[0]) + sh_ref[0]).astype(BF16)
    o_ref[0] = _dot(h, w_ref[...])


def _mod_proj(x, sc, sh, w, tl):
    bx, lx, d = x.shape
    n = w.shape[1]
    sel = _bm(sc)
    return pl.pallas_call(
        _proj_kernel,
        grid=(bx, lx // tl),
        in_specs=[
            pl.BlockSpec((1, tl, d), lambda b, i: (b, i, 0)),
            pl.BlockSpec((1, 1, d), lambda b, i: (sel(b), 0, 0)),
            pl.BlockSpec((1, 1, d), lambda b, i: (sel(b), 0, 0)),
            pl.BlockSpec((d, n), lambda b, i: (0, 0)),
        ],
        out_specs=pl.BlockSpec((1, tl, n), lambda b, i: (b, i, 0)),
        out_shape=jax.ShapeDtypeStruct((bx, lx, n), F32),
        compiler_params=_cparams("parallel", "parallel"),
        name="mod_proj",
    )(x, sc, sh, w)


def _store_packed(o_ref, pairs):
    for q, (xa, xb) in enumerate(pairs):
        for h in range(xa.shape[1] // HEAD):
            sl = slice(HEAD * h, HEAD * (h + 1))
            o_ref[0, h, q] = jnp.concatenate([xa[:, sl], xb[:, sl]], axis=-1)


def _rwkv_prep_kernel(p_ref, pp_ref, pn_ref, sw_ref, w0_ref, a0_ref, kkp_ref, ka_ref, rk_ref,
                      w2h_ref, w2l_ref, a2h_ref, a2l_ref, g2h_ref, g2l_ref, ones_ref,
                      pkf_o, pkb_o, bonus_o, g_o, *, seq_len, width):
    tl = p_ref.shape[1]
    i = pl.program_id(1)
    pc = _conv3(p_ref[0], pp_ref[0, SUBLANES - 1:SUBLANES, :], pn_ref[0, 0:1, :],
                sw_ref[...], i * tl, seq_len)
    w = width
    r = pc[:, 0:w]
    k = pc[:, w:2 * w]
    v = pc[:, 2 * w:3 * w]
    wd = pc[:, 3 * w:3 * w + LANES]
    ad = pc[:, 3 * w + LANES:3 * w + 2 * LANES]
    gd = pc[:, 3 * w + 2 * LANES:3 * w + 3 * LANES]
    ones_bd = ones_ref[...]

    g_o[0] = _dot3(_sigmoid(gd), g2h_ref[...], g2l_ref[...])
    kkr = k * kkp_ref[...]
    kk = kkr * lax.rsqrt(_segsum(kkr * kkr, ones_bd) + 1e-12)
    lw = _dot3(jnp.tanh(wd), w2h_ref[...], w2l_ref[...])
    la = _dot3(ad, a2h_ref[...], a2l_ref[...])

    rrk = r * rk_ref[...]
    bonus = jnp.zeros_like(r)
    for d, pk_o in enumerate((pkf_o, pkb_o)):
        z = w0_ref[d:d + 1, :] + lw[:, d * w:(d + 1) * w]
        w_log = -(jnp.maximum(-z, 0.0) + jnp.log(1.0 + jnp.exp(-jnp.abs(z)))) - 0.5
        decay = jnp.exp(-jnp.exp(w_log))
        a = _sigmoid(a0_ref[d:d + 1, :] + la[:, d * w:(d + 1) * w])
        kd = k * (1.0 + (a - 1.0) * ka_ref[...])
        _store_packed(pk_o, ((r, v), (kk, decay), (kd, kk * a)))
        bonus = bonus + _segsum(rrk * kd, ones_bd) * v
    bonus_o[0] = bonus


def _rwkv_prep(p_r, lp, tl, seq_len):
    b, l, wp = p_r.shape
    w = lp["rwkv_w"]
    nh = w // HEAD
    hm = jax.ShapeDtypeStruct((b, nh, WKV_TILES, l, LANES), F32)
    tok = jax.ShapeDtypeStruct((b, l, w), F32)
    prev, nxt = _halo_specs(tl, wp, l)

    def full(a):
        return pl.BlockSpec(a.shape, lambda b_, i: (0,) * a.ndim)

    consts = [lp["rwkv_shift"], lp["rwkv_w0"], lp["rwkv_a0"], lp["rwkv_kk"], lp["rwkv_ka"],
              lp["rwkv_rk"], lp["w2_hi"], lp["w2_lo"], lp["a2_hi"], lp["a2_lo"],
              lp["g2_hi"], lp["g2_lo"], lp["ones_head"]]
    hm_spec = pl.BlockSpec((1, nh, WKV_TILES, tl, LANES), lambda b_, i: (b_, 0, 0, i, 0))
    tok_spec = pl.BlockSpec((1, tl, w), lambda b_, i: (b_, i, 0))
    return pl.pallas_call(
        functools.partial(_rwkv_prep_kernel, seq_len=seq_len, width=w),
        grid=(b, l // tl),
        in_specs=[pl.BlockSpec((1, tl, wp), lambda b_, i: (b_, i, 0)), prev, nxt]
        + [full(a) for a in consts],
        out_specs=[hm_spec] * 2 + [tok_spec] * 2,
        out_shape=[hm] * 2 + [tok] * 2,
        compiler_params=_cparams("parallel", "parallel"),
        name="rwkv_prep",
    )(p_r, p_r, p_r, *consts)


def _allsum8(x):
    s = x[0:8]
    for q in range(1, x.shape[0] // SUBLANES):
        s = s + x[q * SUBLANES:(q + 1) * SUBLANES]
    s = s + pltpu.roll(s, 4, 0)
    s = s + pltpu.roll(s, 2, 0)
    return s + pltpu.roll(s, 1, 0)


def _wkv_kernel(pf_ref, pb_ref, s0_ref, yf_ref, yb_ref, so_ref, S, OPS, YS, ACC, GAM):
    i = pl.program_id(1)
    n = pl.num_programs(1)
    t_steps = pf_ref.shape[3]
    stride = WKV_TILES * t_steps
    pf2 = pf_ref.reshape(HEAD * stride, LANES)
    pb2 = pb_ref.reshape(HEAD * stride, LANES)
    yf2 = yf_ref.reshape(HEAD * t_steps, LANES)
    yb2 = yb_ref.reshape(HEAD * t_steps, LANES)

    @pl.when(i == 0)
    def _():
        S[...] = jnp.concatenate([s0_ref[0], s0_ref[1]], axis=0).T

    V0, BT0, KDT0, KKT0, RT0 = (q * HEAD for q in range(5))
    BR0, KR0 = 5 * HEAD, 5 * HEAD + SUBLANES

    gam = jnp.ones((HEAD, LANES), F32)
    for t in range(t_steps):
        tiles = []
        for q in range(WKV_TILES):
            f = pf2[pl.ds(q * t_steps + t, HEAD, stride=stride), :]
            b = pb2[pl.ds(q * t_steps + (t_steps - 1 - t), HEAD, stride=stride), :]
            tiles.append(jnp.concatenate([f, b], axis=0).T)
        r, vv = tiles[0][:HEAD], tiles[0][HEAD:]
        kk, w = tiles[1][:HEAD], tiles[1][HEAD:]
        kd, b = tiles[2][:HEAD], tiles[2][HEAD:]
        OPS[t, KKT0:KKT0 + HEAD, :] = kk * gam
        gam = gam * w
        inv = 1.0 / gam
        OPS[t, V0:V0 + HEAD, :] = vv
        OPS[t, BT0:BT0 + HEAD, :] = b * inv
        OPS[t, KDT0:KDT0 + HEAD, :] = kd * inv
        OPS[t, RT0:RT0 + HEAD, :] = r * gam
        OPS[t, BR0:BR0 + SUBLANES, :] = _allsum8(b * r)
        OPS[t, KR0:KR0 + SUBLANES, :] = _allsum8(kd * r)
    GAM[...] = gam

    def accumulate(j, state_tile, k, sa, y0):
        return (sa + state_tile * OPS[j, KKT0 + k:KKT0 + k + 1, :],
                y0 + state_tile * OPS[j, RT0 + k:RT0 + k + 1, :])

    sa = jnp.zeros((HEAD, LANES), F32)
    y0 = jnp.zeros((HEAD, LANES), F32)
    for k in range(HEAD):
        sa, y0 = accumulate(0, S[k * HEAD:(k + 1) * HEAD, :], k, sa, y0)
    ACC[0:HEAD, :] = sa
    ACC[HEAD:2 * HEAD, :] = y0

    def step(j, carry):
        jn = jnp.minimum(j + 1, t_steps - 1)
        sa = ACC[0:HEAD, :]
        y0 = ACC[HEAD:2 * HEAD, :]
        vv = OPS[j, V0:V0 + HEAD, :]
        br = jnp.tile(OPS[j, BR0:BR0 + SUBLANES, :], (HEAD // SUBLANES, 1))
        kr = jnp.tile(OPS[j, KR0:KR0 + SUBLANES, :], (HEAD // SUBLANES, 1))
        YS[j] = y0 - sa * br + vv * kr
        san = jnp.zeros((HEAD, LANES), F32)
        y0n = jnp.zeros((HEAD, LANES), F32)
        for k in range(HEAD):
            rows = slice(k * HEAD, (k + 1) * HEAD)
            snew = (S[rows, :] - sa * OPS[j, BT0 + k:BT0 + k + 1, :]
                    + vv * OPS[j, KDT0 + k:KDT0 + k + 1, :])
            S[rows, :] = snew
            san, y0n = accumulate(jn, snew, k, san, y0n)
        ACC[0:HEAD, :] = san
        ACC[HEAD:2 * HEAD, :] = y0n
        return carry

    lax.fori_loop(0, t_steps, step, 0)

    for k in range(HEAD):
        rows = slice(k * HEAD, (k + 1) * HEAD)
        S[rows, :] = S[rows, :] * GAM[k:k + 1, :]

    zeros = jnp.zeros((HEAD, LANES), F32)
    for t in range(t_steps):
        yt = jnp.concatenate([YS[t], zeros], axis=0).T
        yf2[pl.ds(t, HEAD, stride=t_steps), :] = yt[:HEAD]
        yb2[pl.ds(t_steps - 1 - t, HEAD, stride=t_steps), :] = yt[HEAD:]

    @pl.when(i == n - 1)
    def _():
        st = S[...].T
        so_ref[0] = st[0:HEAD]
        so_ref[1] = st[HEAD:2 * HEAD]


def _wkv_scan(pkf, pkb, s0):
    b, nh, _, l, _ = pkf.shape
    gb = HEAD // nh
    assert gb * nh == HEAD and b % gb == 0 and l % SCAN_T == 0
    groups = b // gb
    n = l // SCAN_T
    pk_blk = (gb, nh, WKV_TILES, SCAN_T, LANES)
    y_blk = (gb, nh, SCAN_T, LANES)
    st = pl.BlockSpec((2, HEAD, HEAD * HEAD), lambda g, i: (0, g, 0))
    y_shape = jax.ShapeDtypeStruct((b, nh, l, LANES), F32)
    return pl.pallas_call(
        _wkv_kernel,
        grid=(groups, n),
        in_specs=[pl.BlockSpec(pk_blk, lambda g, i: (g, 0, 0, i, 0)),
                  pl.BlockSpec(pk_blk, lambda g, i: (g, 0, 0, n - 1 - i, 0)), st],
        out_specs=[pl.BlockSpec(y_blk, lambda g, i: (g, 0, i, 0)),
                   pl.BlockSpec(y_blk, lambda g, i: (g, 0, n - 1 - i, 0)), st],
        out_shape=[y_shape, y_shape, jax.ShapeDtypeStruct((2, b * nh, HEAD * HEAD), F32)],
        scratch_shapes=[pltpu.VMEM((HEAD * HEAD, LANES), F32),
                        pltpu.VMEM((SCAN_T, 5 * HEAD + 2 * SUBLANES, LANES), F32),
                        pltpu.VMEM((SCAN_T, HEAD, LANES), F32),
                        pltpu.VMEM((2 * HEAD, LANES), F32), pltpu.VMEM((HEAD, LANES), F32)],
        compiler_params=_cparams("parallel", "arbitrary"),
        name="wkv_scan",
    )(pkf, pkb, s0)


def _rwkv_post_kernel(yf_ref, yb_ref, bonus_ref, g_ref, gng_ref, gnb_ref, o_ref):
    y = yf_ref[0, :, :, 0:HEAD] + yb_ref[0, :, :, 0:HEAD]
    mu = jnp.mean(y, axis=-1, keepdims=True)
    yc = y - mu
    var = jnp.mean(yc * yc, axis=-1, keepdims=True)
    yn = yc * lax.rsqrt(var + GN_EPS)
    yt = jnp.concatenate([yn[h] for h in range(yn.shape[0])], axis=-1)
    o_ref[0] = (yt * gng_ref[...] + gnb_ref[...] + bonus_ref[0]) * g_ref[0]


def _rwkv_post(yf, yb, bonus, g, gn_g, gn_b, tl):
    b, nh, l, _ = yf.shape
    w = nh * HEAD
    hm = pl.BlockSpec((1, nh, tl, LANES), lambda b_, i: (b_, 0, i, 0))
    tok = pl.BlockSpec((1, tl, w), lambda b_, i: (b_, i, 0))
    vec = pl.BlockSpec((1, w), lambda b_, i: (0, 0))
    return pl.pallas_call(
        _rwkv_post_kernel,
        grid=(b, l // tl),
        in_specs=[hm, hm, tok, tok, vec, vec],
        out_specs=tok,
        out_shape=jax.ShapeDtypeStruct((b, l, w), F32),
        compiler_params=_cparams("parallel", "parallel"),
        name="rwkv_post",
    )(yf, yb, bonus, g, gn_g, gn_b)


def _hy_prep_kernel(p_ref, pp_ref, pn_ref, sw_ref, x0_o, u_o, ub_o, *, seq_len, width):
    tl = p_ref.shape[1]
    i = pl.program_id(1)
    pc = _conv3(p_ref[0], pp_ref[0, SUBLANES - 1:SUBLANES, :], pn_ref[0, 0:1, :],
                sw_ref[...], i * tl, seq_len)
    w = width
    x0_o[0] = pc[:, 0:w]
    u = pc[:, w:2 * w] * pc[:, 2 * w:3 * w]
    u_o[0] = u
    ub_o[0] = u.astype(BF16)


def _hy_prep(p_h, short_w, tl, seq_len):
    b, l, w3 = p_h.shape
    w = w3 // 3
    prev, nxt = _halo_specs(tl, w3, l)
    tok = pl.BlockSpec((1, tl, w), lambda b_, i: (b_, i, 0))
    return pl.pallas_call(
        functools.partial(_hy_prep_kernel, seq_len=seq_len, width=w),
        grid=(b, l // tl),
        in_specs=[pl.BlockSpec((1, tl, w3), lambda b_, i: (b_, i, 0)), prev, nxt,
                  pl.BlockSpec((3, w3), lambda b_, i: (0, 0))],
        out_specs=[tok, tok, tok],
        out_shape=[jax.ShapeDtypeStruct((b, l, w), F32), jax.ShapeDtypeStruct((b, l, w), F32),
                   jax.ShapeDtypeStruct((b, l, w), BF16)],
        compiler_params=_cparams("parallel", "parallel"),
        name="hy_prep",
    )(p_h, p_h, p_h, short_w)


def _hy_mlp_kernel(z_ref, w1_ref, b1_ref, f_ref, w2_ref, b2_ref, w3_ref, dec_ref, h_o):
    z = z_ref[...]
    t01 = z[:, 0:1]
    w1h, w1l = _split2(w1_ref[...])
    w2h, w2l = _split2(w2_ref[...])
    w3h, w3l = _split2(w3_ref[...])
    h = jnp.sin(f_ref[0:1, :] * (_dot3(z, w1h, w1l) + b1_ref[...]))
    h = jnp.sin(f_ref[1:2, :] * (_dot3(h, w2h, w2l) + b2_ref[...]))
    h = _dot3(h, w3h, w3l) * jnp.exp(-t01 * jnp.abs(dec_ref[...]))
    h_o[0] = h.astype(BF16)


def _hy_mlp(z, lp, tl):
    n = z.shape[0]
    c2 = lp["hy_w3"].shape[1]

    def full(a):
        return pl.BlockSpec(a.shape, lambda i: (0,) * a.ndim)

    consts = [lp["hy_w1"], lp["hy_b1"], lp["hy_freq"], lp["hy_w2"], lp["hy_b2"], lp["hy_w3"],
              lp["hy_decay"]]
    out = pl.BlockSpec((1, tl, c2), lambda i: (0, i, 0))
    return pl.pallas_call(
        _hy_mlp_kernel,
        grid=(n // tl,),
        in_specs=[pl.BlockSpec((tl, LANES), lambda i: (i, 0))] + [full(a) for a in consts],
        out_specs=out,
        out_shape=jax.ShapeDtypeStruct((1, n, c2), BF16),
        compiler_params=_cparams("parallel"),
        name="hy_mlp",
    )(z, *consts)


def _dft_filter_kernel(fr_ref, fi_ref, x_ref, hre_o, him_o):
    x = x_ref[0]
    w = x.shape[1] // 2
    xre = _dot(fr_ref[...], x)
    xim = _dot(fi_ref[...], x)
    hre_o[...] = xre[:, :w] + xre[:, w:]
    row = lax.broadcasted_iota(jnp.int32, (xre.shape[0], 1), 0) + pl.program_id(0) * xre.shape[0]
    him_o[...] = jnp.where(row == 0, xim[:, :w] + xim[:, w:], xim[:, :w] - xim[:, w:])


def _dft_filter(mats, h, tf):
    n = h.shape[1]
    w = h.shape[2] // 2
    ft = pl.BlockSpec((tf, n), lambda i: (i, 0))
    xs = pl.BlockSpec((1, n, 2 * w), lambda i: (0, 0, 0))
    out = pl.BlockSpec((tf, w), lambda i: (i, 0))
    return pl.pallas_call(
        _dft_filter_kernel,
        grid=(n // tf,),
        in_specs=[ft, ft, xs],
        out_specs=[out, out],
        out_shape=[jax.ShapeDtypeStruct((n, w), F32)] * 2,
        compiler_params=_cparams("parallel"),
        name="dft_filter",
    )(mats["fre"], mats["fim"], h)


def _dft_signal_kernel(fr_ref, fi_ref, x_ref, hre_ref, him_ref, pr_o, pi_o):
    x = x_ref[0]
    xre = _dot(fr_ref[...], x)
    xim = _dot(fi_ref[...], x)
    hre, him = hre_ref[...], him_ref[...]
    row = lax.broadcasted_iota(jnp.int32, (xre.shape[0], 1), 0) + pl.program_id(1) * xre.shape[0]
    pr_o[0] = jnp.where(row == 0, xre * hre, xre * hre - xim * him).astype(BF16)
    pi_o[0] = jnp.where(row == 0, xim * him, xre * him + xim * hre).astype(BF16)


def _dft_signal(mats, ub, hre, him, tf):
    b, n, w = ub.shape
    ft = pl.BlockSpec((tf, n), lambda b_, i: (i, 0))
    xs = pl.BlockSpec((1, n, w), lambda b_, i: (b_, 0, 0))
    hs = pl.BlockSpec((tf, w), lambda b_, i: (i, 0))
    out = pl.BlockSpec((1, tf, w), lambda b_, i: (b_, i, 0))
    return pl.pallas_call(
        _dft_signal_kernel,
        grid=(b, n // tf),
        in_specs=[ft, ft, xs, hs, hs],
        out_specs=[out] * 2,
        out_shape=[jax.ShapeDtypeStruct((b, n, w), BF16)] * 2,
        compiler_params=_cparams("parallel", "parallel"),
        name="dft_signal",
    )(mats["fre"], mats["fim"], ub, hre, him)


def _dft_inverse_kernel(gr_ref, gi_ref, pr_ref, pi_ref, x0_ref, u_ref, bias_ref, o_ref):
    y = _dot(gr_ref[...], pr_ref[0]) + _dot(gi_ref[...], pi_ref[0])
    o_ref[0] = x0_ref[0] * (y + u_ref[0] * bias_ref[...])


def _dft_inverse(mats, pr, pi, x0, u, bias, tt):
    b, n, w = u.shape
    gt = pl.BlockSpec((tt, n), lambda b_, i: (i, 0))
    ps = pl.BlockSpec((1, n, w), lambda b_, i: (b_, 0, 0))
    tok = pl.BlockSpec((1, tt, w), lambda b_, i: (b_, i, 0))
    return pl.pallas_call(
        _dft_inverse_kernel,
        grid=(b, n // tt),
        in_specs=[gt, gt, ps, ps, tok, tok, pl.BlockSpec((1, w), lambda b_, i: (0, 0))],
        out_specs=tok,
        out_shape=jax.ShapeDtypeStruct((b, n, w), F32),
        compiler_params=_cparams("parallel", "parallel"),
        name="dft_inverse",
    )(mats["gre"], mats["gim"], pr, pi, x0, u, bias)


def _swap_halves(x):
    lane = lax.broadcasted_iota(jnp.int32, x.shape, 1)
    half = HEAD // 2
    return jnp.where(lane % HEAD < half, pltpu.roll(x, LANES - half, 1), pltpu.roll(x, half, 1))


def _attn_prep_kernel(*refs, rope, wq, wkv):
    if rope:
        p_ref, qn_ref, kn_ref, ones_ref, cos_ref, sin_ref, q_o, k_o, v_o, kt_o = refs
    else:
        p_ref, qn_ref, kn_ref, ones_ref, q_o, k_o, v_o, kt_o = refs
    ones_bd = ones_ref[...]

    def norm_rope(x, gain):
        ms = _segsum(x * x, ones_bd) * (1.0 / HEAD)
        xn = x * lax.rsqrt(ms + QK_EPS) * gain
        if rope:
            xn = xn * cos_ref[...] + _swap_halves(xn) * sin_ref[...]
        return xn

    scale = HEAD ** -0.5
    for c in range(wq // LANES):
        xq = norm_rope(p_ref[0, :, c * LANES:(c + 1) * LANES], qn_ref[...]) * scale
        q_o[0, 2 * c] = xq[:, :HEAD].astype(BF16)
        q_o[0, 2 * c + 1] = xq[:, HEAD:].astype(BF16)
    for c in range(wkv // LANES):
        x = p_ref[0, :, wq + c * LANES:wq + (c + 1) * LANES]
        ms = _segsum(x * x, ones_bd) * (1.0 / HEAD)
        xk = x * lax.rsqrt(ms + QK_EPS) * kn_ref[...]
        kt_o[0, :, c * LANES:(c + 1) * LANES] = xk
        if rope:
            xk = xk * cos_ref[...] + _swap_halves(xk) * sin_ref[...]
        k_o[0, 2 * c] = xk[:, :HEAD].astype(BF16)
        k_o[0, 2 * c + 1] = xk[:, HEAD:].astype(BF16)
        xv = p_ref[0, :, wq + wkv + c * LANES:wq + wkv + (c + 1) * LANES]
        v_o[0, 2 * c] = xv[:, :HEAD].astype(BF16)
        v_o[0, 2 * c + 1] = xv[:, HEAD:].astype(BF16)


def _attn_prep(p_a, lp, rope_tabs, tl, wq, wkv):
    b, l, wa = p_a.shape
    nq, nkv = wq // HEAD, wkv // HEAD
    rope = rope_tabs is not None
    vec = pl.BlockSpec((1, LANES), lambda b_, i: (0, 0))
    in_specs = [pl.BlockSpec((1, tl, wa), lambda b_, i: (b_, i, 0)), vec, vec,
                pl.BlockSpec((LANES, LANES), lambda b_, i: (0, 0))]
    args = [p_a, lp["attn_qn"], lp["attn_kn"], lp["ones_pair"]]
    if rope:
        tab = pl.BlockSpec((tl, LANES), lambda b_, i: (i, 0))
        in_specs += [tab, tab]
        args += list(rope_tabs)
    return pl.pallas_call(
        functools.partial(_attn_prep_kernel, rope=rope, wq=wq, wkv=wkv),
        grid=(b, l // tl),
        in_specs=in_specs,
        out_specs=[pl.BlockSpec((1, nq, tl, HEAD), lambda b_, i: (b_, 0, i, 0)),
                   pl.BlockSpec((1, nkv, tl, HEAD), lambda b_, i: (b_, 0, i, 0)),
                   pl.BlockSpec((1, nkv, tl, HEAD), lambda b_, i: (b_, 0, i, 0)),
                   pl.BlockSpec((1, tl, wkv), lambda b_, i: (b_, i, 0))],
        out_shape=[jax.ShapeDtypeStruct((b, nq, l, HEAD), BF16),
                   jax.ShapeDtypeStruct((b, nkv, l, HEAD), BF16),
                   jax.ShapeDtypeStruct((b, nkv, l, HEAD), BF16),
                   jax.ShapeDtypeStruct((b, l, wkv), F32)],
        compiler_params=_cparams("parallel", "parallel"),
        name="attn_prep",
    )(*args)


def _attn_kernel(q_ref, k_ref, v_ref, o_ref):
    k, v = k_ref[0, 0], v_ref[0, 0]
    outs = []
    for h in range(q_ref.shape[1]):
        s = lax.dot_general(q_ref[0, h], k, (((1,), (1,)), ((), ())), preferred_element_type=F32)
        m = jnp.max(s, axis=-1, keepdims=True)
        p = jnp.exp(s - m)
        den = jnp.sum(p, axis=-1, keepdims=True)
        outs.append(_dot(p.astype(BF16), v) / den)
    o_ref[0] = jnp.concatenate(outs, axis=-1)


def _attention(q, k, v, tq):
    b, nq, l, _ = q.shape
    nkv, lk = k.shape[1], k.shape[2]
    g = nq // nkv
    kv = pl.BlockSpec((1, 1, lk, HEAD), lambda b_, h, i: (b_, h, 0, 0))
    return pl.pallas_call(
        _attn_kernel,
        grid=(b, nkv, l // tq),
        in_specs=[pl.BlockSpec((1, g, tq, HEAD), lambda b_, h, i: (b_, h, i, 0)), kv, kv],
        out_specs=pl.BlockSpec((1, tq, g * HEAD), lambda b_, h, i: (b_, i, h)),
        out_shape=jax.ShapeDtypeStruct((b, l, nq * HEAD), F32),
        compiler_params=_cparams("parallel", "parallel", "parallel"),
        name="attention",
    )(q, k, v)


def _out_proj_kernel(yr_ref, yh_ref, ya_ref, x_ref, g_ref, w_ref, lng_ref, lnb_ref, o_ref, *, alpha):
    wr, wh = yr_ref.shape[2], yh_ref.shape[2]
    mix = _dot(yr_ref[0].astype(BF16), w_ref[0:wr, :])
    mix = mix + _dot(yh_ref[0].astype(BF16), w_ref[wr:wr + wh, :])
    mix = mix + _dot(ya_ref[0].astype(BF16), w_ref[wr + wh:, :])
    o_ref[0] = _layer_norm(alpha * x_ref[0] + g_ref[0] * mix, lng_ref[...], lnb_ref[...])


def _out_proj(y_r, y_h, y_a, x, gate, w_out, ln_g, ln_b, tl, alpha):
    bx, lx, d = x.shape
    sel = _bm(gate)

    def tok(a):
        return pl.BlockSpec((1, tl, a.shape[2]), lambda b, i: (b, i, 0))

    vec = pl.BlockSpec((1, d), lambda b, i: (0, 0))
    return pl.pallas_call(
        functools.partial(_out_proj_kernel, alpha=alpha),
        grid=(bx, lx // tl),
        in_specs=[tok(y_r), tok(y_h), tok(y_a), tok(x),
                  pl.BlockSpec((1, 1, d), lambda b, i: (sel(b), 0, 0)),
                  pl.BlockSpec(w_out.shape, lambda b, i: (0, 0)), vec, vec],
        out_specs=tok(x),
        out_shape=jax.ShapeDtypeStruct((bx, lx, d), F32),
        compiler_params=_cparams("parallel", "parallel"),
        name="out_proj",
    )(y_r, y_h, y_a, x, gate, w_out, ln_g, ln_b)


def _ffn_kernel(x_ref, xp_ref, xn_ref, sc_ref, sh_ref, g_ref, wa_ref, wb_ref, ca_ref, cb_ref,
                wd_ref, lng_ref, lnb_ref, o_ref, h_s, acc_s, ua_s, ub_s, *, seq_len, alpha):
    tl = x_ref.shape[1]
    i = pl.program_id(1)
    j = pl.program_id(2)
    halo = FFN_HALO

    @pl.when(j == 0)
    def _():
        sc, sh = 1.0 + sc_ref[0], sh_ref[0]
        h_s[0:halo] = (xp_ref[0] * sc + sh).astype(BF16)
        h_s[halo:halo + tl] = (x_ref[0] * sc + sh).astype(BF16)
        h_s[halo + tl:] = (xn_ref[0] * sc + sh).astype(BF16)
        acc_s[...] = jnp.zeros_like(acc_s)

    row = lax.broadcasted_iota(jnp.int32, (tl, 1), 0)
    pos = (i * tl + row) % seq_len
    first = pos == 0
    last = pos == seq_len - 1
    h = h_s[...]

    ua_s[...] = _dot(h, wa_ref[...])
    ub_s[...] = _dot(h, wb_ref[...])

    def conv(u_s, c_ref):
        um = jnp.where(first, 0.0, u_s[halo - 1:halo - 1 + tl, :])
        up = jnp.where(last, 0.0, u_s[halo + 1:halo + 1 + tl, :])
        return um * c_ref[0:1, :] + u_s[halo:halo + tl, :] * c_ref[1:2, :] + up * c_ref[2:3, :]

    a = conv(ua_s, ca_ref)
    b = conv(ub_s, cb_ref)
    f = (a * _sigmoid(a) * b).astype(BF16)
    acc_s[...] += _dot(f, wd_ref[...])

    @pl.when(j == pl.num_programs(2) - 1)
    def _():
        o_ref[0] = _layer_norm(alpha * x_ref[0] + g_ref[0] * acc_s[...], lng_ref[...], lnb_ref[...])


def _ffn(x, sc, sh, gate, w_up, conv_w, w_down, ln_g, ln_b, tl, tn, seq_len, alpha):
    bx, lx, d = x.shape
    dff = w_down.shape[0]
    nj = dff // tn
    sel = _bm(sc)
    prev, nxt = _halo_specs(tl, d, lx, FFN_HALO)
    prev3 = pl.BlockSpec(prev.block_shape, lambda b, i, j: prev.index_map(b, i))
    nxt3 = pl.BlockSpec(nxt.block_shape, lambda b, i, j: nxt.index_map(b, i))
    mod = pl.BlockSpec((1, 1, d), lambda b, i, j: (sel(b), 0, 0))
    vec = pl.BlockSpec((1, d), lambda b, i, j: (0, 0))
    tok = pl.BlockSpec((1, tl, d), lambda b, i, j: (b, i, 0))
    return pl.pallas_call(
        functools.partial(_ffn_kernel, seq_len=seq_len, alpha=alpha),
        grid=(bx, lx // tl, nj),
        in_specs=[tok, prev3, nxt3, mod, mod, mod,
                  pl.BlockSpec((d, tn), lambda b, i, j: (0, j)),
                  pl.BlockSpec((d, tn), lambda b, i, j: (0, j + nj)),
                  pl.BlockSpec((3, tn), lambda b, i, j: (0, j)),
                  pl.BlockSpec((3, tn), lambda b, i, j: (0, j + nj)),
                  pl.BlockSpec((tn, d), lambda b, i, j: (j, 0)), vec, vec],
        out_specs=tok,
        out_shape=jax.ShapeDtypeStruct((bx, lx, d), F32),
        scratch_shapes=[pltpu.VMEM((tl + 2 * FFN_HALO, d), BF16), pltpu.VMEM((tl, d), F32),
                        pltpu.VMEM((tl + 2 * FFN_HALO, tn), F32),
                        pltpu.VMEM((tl + 2 * FFN_HALO, tn), F32)],
        compiler_params=_cparams("parallel", "parallel", "arbitrary"),
        name="conv_ffn",
    )(x, x, x, sc, sh, gate, w_up, w_up, conv_w, conv_w, w_down, ln_g, ln_b)


def _split2_host(x):
    hi = lax.bitcast_convert_type(lax.bitcast_convert_type(x, jnp.uint32) & jnp.uint32(0xFFFF0000), F32)
    return hi.astype(BF16), (x - hi).astype(BF16)


def _dft_mats(n):
    big = 2 * n
    lo = min(n, 64)
    k = jnp.arange(n, dtype=jnp.int32)[:, None]

    def table(t):
        ang = ((k * t[None, :]) % big).astype(F32) * (2.0 * math.pi / big)
        return jnp.cos(ang), jnp.sin(ang)

    ca, sa = table(lo * jnp.arange(n // lo, dtype=jnp.int32))
    cb, sb = table(jnp.arange(lo, dtype=jnp.int32))
    cos = (ca[:, :, None] * cb[:, None, :] - sa[:, :, None] * sb[:, None, :]).reshape(n, n)
    sin = (sa[:, :, None] * cb[:, None, :] + ca[:, :, None] * sb[:, None, :]).reshape(n, n)
    t = jnp.arange(n, dtype=jnp.int32)[None, :]
    fre = cos
    fim = jnp.where(k == 0, jnp.where(t % 2 == 0, 1.0, -1.0), -sin)
    scale = jnp.where(k == 0, 1.0 / big, 2.0 / big)
    return {"fre": fre.astype(BF16), "fim": fim.astype(BF16),
            "gre": (fre * scale).T.astype(BF16), "gim": (fim * scale).T.astype(BF16)}


def _hyena_features(n, n_bands):
    t01 = jnp.linspace(0.0, 1.0, n, dtype=F32)[:, None]
    pos = jnp.arange(n, dtype=F32)[:, None]
    bands = jnp.linspace(1e-4, n_bands - 1, n_bands, dtype=F32)[None, :]
    ang = (2.0 * math.pi / n) * pos * bands
    z = jnp.concatenate([t01, jnp.cos(ang), -jnp.sin(ang)], -1)
    return jnp.pad(z, ((0, 0), (0, LANES - z.shape[1])))


def _rope_tables(n_tokens):
    rows = n_tokens // GRID_W
    row = jnp.repeat(jnp.arange(rows, dtype=F32), GRID_W)
    col = jnp.tile(jnp.arange(GRID_W, dtype=F32), rows)
    n_freq = HEAD // 4
    inv = ROPE_THETA ** (-jnp.arange(n_freq, dtype=F32) / n_freq)
    ang = jnp.concatenate([row[:, None] * inv, col[:, None] * inv], -1)
    cos, sin = jnp.cos(ang), jnp.sin(ang)
    cos2 = jnp.tile(jnp.concatenate([cos, cos], -1), (1, LANES // HEAD))
    sin2 = jnp.tile(jnp.concatenate([-sin, sin], -1), (1, LANES // HEAD))
    return cos2, sin2


def _block_ones(n, group):
    idx = jnp.arange(n) // group
    return (idx[:, None] == idx[None, :]).astype(BF16)


def _pad_to(a, shape):
    return jnp.pad(a, [(0, s - d) for d, s in zip(a.shape, shape)])


def _block_diag2(a, b):
    za = jnp.zeros((a.shape[0], b.shape[1]), a.dtype)
    zb = jnp.zeros((b.shape[0], a.shape[1]), a.dtype)
    return jnp.concatenate([jnp.concatenate([a, za], 1), jnp.concatenate([zb, b], 1)], 0)


def _layer_params(l, P, dims):
    w_r, w_h, wq, wkv = dims["w_rwkv"], dims["w_hyena"], dims["wq"], dims["wkv"]
    rwkv_cols = 3 * w_r + 2 * P["rwkv_w2"].shape[2] + 2 * P["rwkv_a2"].shape[2] + P["rwkv_g2"].shape[1]
    rwkv_pad = -(-rwkv_cols // (2 * LANES)) * (2 * LANES)
    hy_cols = 3 * w_h
    w_in = P["w_in"][l]
    lp = {"rwkv_w": w_r}
    lp["w_in_r"] = _pad_to(w_in[:, :rwkv_cols], (w_in.shape[0], rwkv_pad)).astype(BF16)
    lp["w_in_h"] = w_in[:, rwkv_cols:rwkv_cols + hy_cols].astype(BF16)
    lp["w_in_a"] = w_in[:, rwkv_cols + hy_cols:].astype(BF16)
    lp["rwkv_shift"] = _pad_to(P["rwkv_shift"][l], (3, rwkv_pad))
    lp["rwkv_w0"] = P["rwkv_w0"][l]
    lp["rwkv_a0"] = P["rwkv_a0"][l]
    for nm in ("rwkv_kk", "rwkv_ka", "rwkv_rk", "rwkv_gn_g", "rwkv_gn_b", "hy_bias",
               "ln1_g", "ln1_b", "ln2_g", "ln2_b"):
        lp[nm] = P[nm][l][None, :]
    lp["w2_hi"], lp["w2_lo"] = _split2_host(_block_diag2(P["rwkv_w2"][l, 0], P["rwkv_w2"][l, 1]))
    lp["a2_hi"], lp["a2_lo"] = _split2_host(_block_diag2(P["rwkv_a2"][l, 0], P["rwkv_a2"][l, 1]))
    lp["g2_hi"], lp["g2_lo"] = _split2_host(P["rwkv_g2"][l])
    lp["ones_head"] = _block_ones(w_r, HEAD)
    lp["ones_pair"] = _block_ones(LANES, HEAD)
    lp["hy_short"] = P["hy_short"][l]
    ffn_w = P["hy_w1"].shape[2]
    lp["hy_w1"] = _pad_to(P["hy_w1"][l], (LANES, LANES))
    lp["hy_b1"] = _pad_to(P["hy_b1"][l][None, :], (1, LANES))
    lp["hy_freq"] = _pad_to(P["hy_freq"][l], (2, LANES))
    lp["hy_w2"] = _pad_to(P["hy_w2"][l], (LANES, LANES))
    lp["hy_b2"] = _pad_to(P["hy_b2"][l][None, :], (1, LANES))
    lp["hy_w3"] = _pad_to(P["hy_w3"][l], (LANES, 2 * w_h))
    lp["hy_decay"] = P["hy_decay"][l].reshape(1, 2 * w_h)
    del ffn_w
    lp["attn_qn"] = jnp.tile(P["attn_qn"][l], LANES // HEAD)[None, :]
    lp["attn_kn"] = jnp.tile(P["attn_kn"][l], LANES // HEAD)[None, :]
    lp["w_out"] = P["w_out"][l].astype(BF16)
    lp["ffn_up"] = P["ffn_up"][l].astype(BF16)
    lp["ffn_conv"] = P["ffn_conv"][l]
    lp["ffn_down"] = P["ffn_down"][l].astype(BF16)
    return lp


def _trunk_layer(x, mod6, lp, dims, rope_tabs, ctx_kv, s0, dft, z_feat, tiles):
    b, l, d = x.shape
    sh1, sc1, g1, sh2, sc2, g2 = mod6
    shared = sh1.shape[0] == 1
    alpha = dims["alpha"]
    w_r, w_h, wq, wkv = dims["w_rwkv"], dims["w_hyena"], dims["wq"], dims["wkv"]
    nh = w_r // HEAD
    tl_mm, tl_ew = tiles["mm"], tiles["ew"]

    xm = x.reshape(1, b * l, d) if shared else x
    p_r = _mod_proj(xm, sc1, sh1, lp["w_in_r"], tl_mm).reshape(b, l, -1)
    p_h = _mod_proj(xm, sc1, sh1, lp["w_in_h"], tl_mm).reshape(b, l, -1)
    p_a = _mod_proj(xm, sc1, sh1, lp["w_in_a"], tl_mm).reshape(b, l, -1)

    pkf, pkb, bonus, gate = _rwkv_prep(p_r, lp, tl_ew, l)
    yf, yb, s_fin = _wkv_scan(pkf, pkb, s0)
    y_r = _rwkv_post(yf, yb, bonus, gate, lp["rwkv_gn_g"], lp["rwkv_gn_b"], tl_ew)

    x0, u, ub = _hy_prep(p_h, lp["hy_short"], tl_ew, l)
    tf = min(l, 256)
    hre, him = _dft_filter(dft, _hy_mlp(z_feat, lp, tf), tf)
    pr, pi = _dft_signal(dft, ub, hre, him, tf)
    y_h = _dft_inverse(dft, pr, pi, x0, u, lp["hy_bias"], tf)

    q, k, vv, k_tok = _attn_prep(p_a, lp, rope_tabs, tl_ew, wq, wkv)
    if ctx_kv is not None:
        ck, cv = ctx_kv
        k = jnp.concatenate([k, jnp.swapaxes(ck, 1, 2).astype(BF16)], axis=2)
        vv = jnp.concatenate([vv, jnp.swapaxes(cv, 1, 2).astype(BF16)], axis=2)
    y_a = _attention(q, k, vv, tiles["tq"])

    def m(a):
        return a.reshape(1, b * l, a.shape[-1]) if shared else a

    x1 = _out_proj(m(y_r), m(y_h), m(y_a), xm, g1, lp["w_out"], lp["ln1_g"], lp["ln1_b"],
                   tl_mm, alpha)
    x2 = _ffn(x1, sc2, sh2, g2, lp["ffn_up"], lp["ffn_conv"], lp["ffn_down"],
              lp["ln2_g"], lp["ln2_b"], tl_mm, tiles["ffn_tn"], l, alpha)
    v_tok = p_a[..., wq + wkv:]
    return x2.reshape(b, l, d), k_tok, v_tok, s_fin


def kernel(x_prompt, x_sample, cache_k, cache_v, state_rwkv, c, c_ctx, w_mod, b_mod, w_in, rwkv_shift, rwkv_w0, rwkv_w2, rwkv_a0, rwkv_a2, rwkv_kk, rwkv_ka, rwkv_rk, rwkv_g2, rwkv_gn_g, rwkv_gn_b, hy_short, hy_w1, hy_b1, hy_freq, hy_w2, hy_b2, hy_w3, hy_decay, hy_bias, attn_qn, attn_kn, w_out, ln1_g, ln1_b, ln2_g, ln2_b, ffn_up, ffn_conv, ffn_down):
    P = dict(w_in=w_in, rwkv_shift=rwkv_shift, rwkv_w0=rwkv_w0, rwkv_w2=rwkv_w2, rwkv_a0=rwkv_a0,
             rwkv_a2=rwkv_a2, rwkv_kk=rwkv_kk, rwkv_ka=rwkv_ka, rwkv_rk=rwkv_rk, rwkv_g2=rwkv_g2,
             rwkv_gn_g=rwkv_gn_g, rwkv_gn_b=rwkv_gn_b, hy_short=hy_short, hy_w1=hy_w1, hy_b1=hy_b1,
             hy_freq=hy_freq, hy_w2=hy_w2, hy_b2=hy_b2, hy_w3=hy_w3, hy_decay=hy_decay,
             hy_bias=hy_bias, attn_qn=attn_qn, attn_kn=attn_kn, w_out=w_out, ln1_g=ln1_g,
             ln1_b=ln1_b, ln2_g=ln2_g, ln2_b=ln2_b, ffn_up=ffn_up, ffn_conv=ffn_conv,
             ffn_down=ffn_down)
    depth, d = w_mod.shape[0], w_mod.shape[1]
    bc, lc, _ = x_prompt.shape
    bd, ld, _ = x_sample.shape
    nkv = cache_k.shape[3]
    nh = state_rwkv.shape[3]
    w_r = nh * HEAD
    w_h = hy_bias.shape[1]
    wkv = nkv * HEAD
    wq = w_in.shape[2] - (3 * w_r + 2 * rwkv_w2.shape[2] + 2 * rwkv_a2.shape[2] + rwkv_g2.shape[1]) \
        - 3 * w_h - 2 * wkv
    dims = dict(w_rwkv=w_r, w_hyena=w_h, wq=wq, wkv=wkv, alpha=(2 * depth) ** 0.25)

    rows = -(-(bd + 1) // SUBLANES) * SUBLANES
    cond = _pad_to(jnp.concatenate([c, c_ctx[None, :]], 0), (rows, d))
    mod = _modulation(cond, w_mod, b_mod)

    rope_tabs = _rope_tables(ld)
    dft_c, dft_d = _dft_mats(lc), _dft_mats(ld)
    n_bands = (hy_w1.shape[1] - 1) // 2
    z_c, z_d = _hyena_features(lc, n_bands), _hyena_features(ld, n_bands)
    tiles_c = dict(mm=min(512, bc * lc), ew=min(256, lc), tq=min(256, lc), ffn_tn=512)
    tiles_d = dict(mm=min(512, ld), ew=min(256, ld), tq=min(512, ld), ffn_tn=512)

    xp, xs = x_prompt, x_sample
    zero_state = jnp.zeros((2, bc * nh, HEAD * HEAD), F32)
    new_k, new_v, new_s = [], [], []
    for l in range(depth):
        lp = _layer_params(l, P, dims)
        mod_d = [m[:bd, None, :] for m in jnp.split(mod[l], 6, axis=-1)]
        mod_c = [m[bd:bd + 1, None, :] for m in jnp.split(mod[l], 6, axis=-1)]
        xp, k_c, v_c, s_c = _trunk_layer(xp, mod_c, lp, dims, None, None, zero_state, dft_c, z_c, tiles_c)
        new_k.append(k_c.reshape(bc, lc, nkv, HEAD))
        new_v.append(v_c.reshape(bc, lc, nkv, HEAD))
        new_s.append(jnp.transpose(s_c.reshape(2, bc, nh, HEAD, HEAD), (1, 0, 2, 4, 3)))
        s0 = jnp.transpose(state_rwkv[:, l], (1, 0, 2, 4, 3)).reshape(2, bd * nh, HEAD * HEAD)
        xs, _, _, _ = _trunk_layer(xs, mod_d, lp, dims, rope_tabs, (cache_k[:, l], cache_v[:, l]),
                                   s0, dft_d, z_d, tiles_d)
    return (xp, xs, jnp.stack(new_k, axis=1), jnp.stack(new_v, axis=1), jnp.stack(new_s, axis=1))
```

```python
import functools
import math

import jax
import jax.numpy as jnp
from jax import lax
from jax.experimental import pallas as pl
from jax.experimental.pallas import tpu as pltpu

F32 = jnp.float32
BF16 = jnp.bfloat16

HEAD = 64
LANES = 128
SUBLANES = 8
VMEM_LIMIT = 52 * 1024 * 1024
LN_EPS = 1e-5
QK_EPS = 1e-6
GN_EPS = 64e-5
ROPE_THETA = 10000.0
GRID_W = 64
SCAN_T = 16
WKV_TILES = 3
FFN_HALO = 16


def _cparams(*sem):
    return pltpu.CompilerParams(dimension_semantics=sem, vmem_limit_bytes=VMEM_LIMIT)


def _dot(a, b):
    return jnp.dot(a, b, preferred_element_type=F32)


def _hi_f32(x):
    u = pltpu.bitcast(x, jnp.uint32) & jnp.uint32(0xFFFF0000)
    return pltpu.bitcast(u, F32)


def _split2(x):
    h = _hi_f32(x)
    return h.astype(BF16), (x - h).astype(BF16)


def _dot3(a, b_hi, b_lo):
    a_hi, a_lo = _split2(a)
    return _dot(a_hi, b_hi) + (_dot(a_lo, b_hi) + _dot(a_hi, b_lo))


def _segsum(x, ones_bd):
    hi, lo = _split2(x)
    return _dot(hi, ones_bd) + _dot(lo, ones_bd)


def _layer_norm(x, g, b):
    mu = jnp.mean(x, axis=-1, keepdims=True)
    xc = x - mu
    var = jnp.mean(xc * xc, axis=-1, keepdims=True)
    return xc * lax.rsqrt(var + LN_EPS) * g + b


def _sigmoid(x):
    return 1.0 / (1.0 + jnp.exp(-x))


def _conv3(x_ref, prev_ref, next_ref, w, seq_len):
    x = x_ref[0]
    tl = x.shape[0]
    i = pl.program_id(1)
    prev_row = jnp.where((i * tl) % seq_len == 0, 0.0, prev_ref[0, SUBLANES - 1:SUBLANES, :])
    next_row = jnp.where(((i + 1) * tl) % seq_len == 0, 0.0, next_ref[0, 0:1, :])
    row = lax.broadcasted_iota(jnp.int32, (SUBLANES, 1), 0)
    xm = pltpu.roll(x, 1, 0)
    xp = pltpu.roll(x, tl - 1, 0)
    xm = jnp.concatenate([jnp.where(row == 0, prev_row, xm[:SUBLANES]), xm[SUBLANES:]], axis=0)
    xp = jnp.concatenate([xp[:tl - SUBLANES],
                          jnp.where(row == SUBLANES - 1, next_row, xp[tl - SUBLANES:])], axis=0)
    return xm * w[0:1, :] + x * w[1:2, :] + xp * w[2:3, :]


def _halo_specs(tl, width, n_rows, halo=SUBLANES):
    r = tl // halo
    last = n_rows // halo - 1
    prev = pl.BlockSpec((1, halo, width), lambda b, i: (b, jnp.maximum(i * r - 1, 0), 0))
    nxt = pl.BlockSpec((1, halo, width), lambda b, i: (b, jnp.minimum((i + 1) * r, last), 0))
    return prev, nxt


def _bm(arr):
    if arr.shape[0] == 1:
        return lambda b: 0
    return lambda b: b


def _mod_kernel(c_ref, w_ref, b_ref, o_ref):
    c = c_ref[...]
    s = c * _sigmoid(c)
    w_hi, w_lo = _split2(w_ref[0])
    o_ref[0] = _dot3(s, w_hi, w_lo) + b_ref[0]


def _modulation(cond, w_mod, b_mod):
    depth, d, n = w_mod.shape
    rows = cond.shape[0]
    tn = 1024
    return pl.pallas_call(
        _mod_kernel,
        grid=(depth, n // tn),
        in_specs=[
            pl.BlockSpec((rows, d), lambda l, j: (0, 0)),
            pl.BlockSpec((1, d, tn), lambda l, j: (l, 0, j)),
            pl.BlockSpec((1, 1, tn), lambda l, j: (l, 0, j)),
        ],
        out_specs=pl.BlockSpec((1, rows, tn), lambda l, j: (l, 0, j)),
        out_shape=jax.ShapeDtypeStruct((depth, rows, n), F32),
        compiler_params=_cparams("parallel", "parallel"),
        name="modulation",
    )(cond, w_mod, b_mod.reshape(depth, 1, n))


def _proj_kernel(x_ref, sc_ref, sh_ref, w_ref, o_ref):
    h = (x_ref[0] * (1.0 + sc_ref[0]) + sh_ref[0]).astype(BF16)
    o_ref[0] = _dot(h, w_ref[...])


def _mod_proj(x, sc, sh, w, tl):
    bx, lx, d = x.shape
    n = w.shape[1]
    sel = _bm(sc)
    return pl.pallas_call(
        _proj_kernel,
        grid=(bx, lx // tl),
        in_specs=[
            pl.BlockSpec((1, tl, d), lambda b, i: (b, i, 0)),
            pl.BlockSpec((1, 1, d), lambda b, i: (sel(b), 0, 0)),
            pl.BlockSpec((1, 1, d), lambda b, i: (sel(b), 0, 0)),
            pl.BlockSpec((d, n), lambda b, i: (0, 0)),
        ],
        out_specs=pl.BlockSpec((1, tl, n), lambda b, i: (b, i, 0)),
        out_shape=jax.ShapeDtypeStruct((bx, lx, n), F32),
        compiler_params=_cparams("parallel", "parallel"),
        name="mod_proj",
    )(x, sc, sh, w)


def _store_packed(o_ref, pairs):
    for q, (xa, xb) in enumerate(pairs):
        for h in range(xa.shape[1] // HEAD):
            sl = slice(HEAD * h, HEAD * (h + 1))
            o_ref[0, h, q] = jnp.concatenate([xa[:, sl], xb[:, sl]], axis=-1)


def _rwkv_prep_kernel(p_ref, pp_ref, pn_ref, sw_ref, w0_ref, a0_ref, kkp_ref, ka_ref, rk_ref,
                      w2h_ref, w2l_ref, a2h_ref, a2l_ref, g2h_ref, g2l_ref, ones_ref,
                      pkf_o, pkb_o, bonus_o, g_o, *, seq_len, width):
    pc = _conv3(p_ref, pp_ref, pn_ref, sw_ref[...], seq_len)
    w = width
    r = pc[:, 0:w]
    k = pc[:, w:2 * w]
    v = pc[:, 2 * w:3 * w]
    wd = pc[:, 3 * w:3 * w + LANES]
    ad = pc[:, 3 * w + LANES:3 * w + 2 * LANES]
    gd = pc[:, 3 * w + 2 * LANES:3 * w + 3 * LANES]
    ones_bd = ones_ref[...]

    g_o[0] = _dot3(_sigmoid(gd), g2h_ref[...], g2l_ref[...])
    kkr = k * kkp_ref[...]
    kk = kkr * lax.rsqrt(_segsum(kkr * kkr, ones_bd) + 1e-12)
    lw = _dot3(jnp.tanh(wd), w2h_ref[...], w2l_ref[...])
    la = _dot3(ad, a2h_ref[...], a2l_ref[...])

    rrk = r * rk_ref[...]
    bonus = jnp.zeros_like(r)
    for d, pk_o in enumerate((pkf_o, pkb_o)):
        z = w0_ref[d:d + 1, :] + lw[:, d * w:(d + 1) * w]
        decay = jnp.exp(-math.exp(-0.5) * _sigmoid(z))
        a = _sigmoid(a0_ref[d:d + 1, :] + la[:, d * w:(d + 1) * w])
        kd = k * (1.0 + (a - 1.0) * ka_ref[...])
        _store_packed(pk_o, ((r, v), (kk, decay), (kd, kk * a)))
        bonus = bonus + _segsum(rrk * kd, ones_bd) * v
    bonus_o[0] = bonus


def _rwkv_prep(p_r, lp, tl, seq_len):
    b, l, wp = p_r.shape
    assert seq_len % tl == 0
    w = lp["rwkv_w"]
    nh = w // HEAD
    hm = jax.ShapeDtypeStruct((b, nh, WKV_TILES, l, LANES), F32)
    tok = jax.ShapeDtypeStruct((b, l, w), F32)
    prev, nxt = _halo_specs(tl, wp, l)

    def full(a):
        return pl.BlockSpec(a.shape, lambda b_, i: (0,) * a.ndim)

    consts = [lp["rwkv_shift"], lp["rwkv_w0"], lp["rwkv_a0"], lp["rwkv_kk"], lp["rwkv_ka"],
              lp["rwkv_rk"], lp["w2_hi"], lp["w2_lo"], lp["a2_hi"], lp["a2_lo"],
              lp["g2_hi"], lp["g2_lo"], lp["ones_head"]]
    hm_spec = pl.BlockSpec((1, nh, WKV_TILES, tl, LANES), lambda b_, i: (b_, 0, 0, i, 0))
    tok_spec = pl.BlockSpec((1, tl, w), lambda b_, i: (b_, i, 0))
    return pl.pallas_call(
        functools.partial(_rwkv_prep_kernel, seq_len=seq_len, width=w),
        grid=(b, l // tl),
        in_specs=[pl.BlockSpec((1, tl, wp), lambda b_, i: (b_, i, 0)), prev, nxt]
        + [full(a) for a in consts],
        out_specs=[hm_spec] * 2 + [tok_spec] * 2,
        out_shape=[hm] * 2 + [tok] * 2,
        compiler_params=_cparams("parallel", "parallel"),
        name="rwkv_prep",
    )(p_r, p_r, p_r, *consts)


def _allsum8(x):
    s = x[0:8]
    for q in range(1, x.shape[0] // SUBLANES):
        s = s + x[q * SUBLANES:(q + 1) * SUBLANES]
    s = s + pltpu.roll(s, 4, 0)
    s = s + pltpu.roll(s, 2, 0)
    return s + pltpu.roll(s, 1, 0)


def _wkv_kernel(pf_ref, pb_ref, s0_ref, yf_ref, yb_ref, so_ref, S, OPS, YS, ACC, GAM):
    i = pl.program_id(1)
    n = pl.num_programs(1)
    t_steps = pf_ref.shape[3]
    stride = WKV_TILES * t_steps
    pf2 = pf_ref.reshape(HEAD * stride, LANES)
    pb2 = pb_ref.reshape(HEAD * stride, LANES)
    yf2 = yf_ref.reshape(HEAD * t_steps, LANES)
    yb2 = yb_ref.reshape(HEAD * t_steps, LANES)

    @pl.when(i == 0)
    def _():
        S[...] = jnp.concatenate([s0_ref[0], s0_ref[1]], axis=0).T

    V0, BT0, KDT0, KKT0, RT0 = (q * HEAD for q in range(5))
    BR0, KR0 = 5 * HEAD, 5 * HEAD + SUBLANES

    gam = jnp.ones((HEAD, LANES), F32)
    for t in range(t_steps):
        tiles = []
        for q in range(WKV_TILES):
            f = pf2[pl.ds(q * t_steps + t, HEAD, stride=stride), :]
            b = pb2[pl.ds(q * t_steps + (t_steps - 1 - t), HEAD, stride=stride), :]
            tiles.append(jnp.concatenate([f, b], axis=0).T)
        r, vv = tiles[0][:HEAD], tiles[0][HEAD:]
        kk, w = tiles[1][:HEAD], tiles[1][HEAD:]
        kd, b = tiles[2][:HEAD], tiles[2][HEAD:]
        OPS[t, KKT0:KKT0 + HEAD, :] = kk * gam
        gam = gam * w
        inv = 1.0 / gam
        OPS[t, V0:V0 + HEAD, :] = vv
        OPS[t, BT0:BT0 + HEAD, :] = b * inv
        OPS[t, KDT0:KDT0 + HEAD, :] = kd * inv
        OPS[t, RT0:RT0 + HEAD, :] = r * gam
        OPS[t, BR0:BR0 + SUBLANES, :] = _allsum8(b * r)
        OPS[t, KR0:KR0 + SUBLANES, :] = _allsum8(kd * r)
    GAM[...] = gam

    def accumulate(j, state_tile, k, sa, y0):
        return (sa + state_tile * OPS[j, KKT0 + k:KKT0 + k + 1, :],
                y0 + state_tile * OPS[j, RT0 + k:RT0 + k + 1, :])

    sa = jnp.zeros((HEAD, LANES), F32)
    y0 = jnp.zeros((HEAD, LANES), F32)
    for k in range(HEAD):
        sa, y0 = accumulate(0, S[k * HEAD:(k + 1) * HEAD, :], k, sa, y0)
    ACC[0:HEAD, :] = sa
    ACC[HEAD:2 * HEAD, :] = y0

    def step(j, carry):
        jn = jnp.minimum(j + 1, t_steps - 1)
        sa = ACC[0:HEAD, :]
        y0 = ACC[HEAD:2 * HEAD, :]
        vv = OPS[j, V0:V0 + HEAD, :]
        br = jnp.tile(OPS[j, BR0:BR0 + SUBLANES, :], (HEAD // SUBLANES, 1))
        kr = jnp.tile(OPS[j, KR0:KR0 + SUBLANES, :], (HEAD // SUBLANES, 1))
        YS[j] = y0 - sa * br + vv * kr
        san = jnp.zeros((HEAD, LANES), F32)
        y0n = jnp.zeros((HEAD, LANES), F32)
        for k in range(HEAD):
            rows = slice(k * HEAD, (k + 1) * HEAD)
            snew = (S[rows, :] - sa * OPS[j, BT0 + k:BT0 + k + 1, :]
                    + vv * OPS[j, KDT0 + k:KDT0 + k + 1, :])
            S[rows, :] = snew
            san, y0n = accumulate(jn, snew, k, san, y0n)
        ACC[0:HEAD, :] = san
        ACC[HEAD:2 * HEAD, :] = y0n
        return carry

    lax.fori_loop(0, t_steps, step, 0)

    for k in range(HEAD):
        rows = slice(k * HEAD, (k + 1) * HEAD)
        S[rows, :] = S[rows, :] * GAM[k:k + 1, :]

    zeros = jnp.zeros((HEAD, LANES), F32)
    for t in range(t_steps):
        yt = jnp.concatenate([YS[t], zeros], axis=0).T
        yf2[pl.ds(t, HEAD, stride=t_steps), :] = yt[:HEAD]
        yb2[pl.ds(t_steps - 1 - t, HEAD, stride=t_steps), :] = yt[HEAD:]

    @pl.when(i == n - 1)
    def _():
        st = S[...].T
        so_ref[0] = st[0:HEAD]
        so_ref[1] = st[HEAD:2 * HEAD]


def _wkv_scan(pkf, pkb, s0):
    b, nh, _, l, _ = pkf.shape
    gb = HEAD // nh
    assert gb * nh == HEAD and b % gb == 0 and l % SCAN_T == 0
    groups = b // gb
    n = l // SCAN_T
    pk_blk = (gb, nh, WKV_TILES, SCAN_T, LANES)
    y_blk = (gb, nh, SCAN_T, LANES)
    st = pl.BlockSpec((2, HEAD, HEAD * HEAD), lambda g, i: (0, g, 0))
    y_shape = jax.ShapeDtypeStruct((b, nh, l, LANES), F32)
    return pl.pallas_call(
        _wkv_kernel,
        grid=(groups, n),
        in_specs=[pl.BlockSpec(pk_blk, lambda g, i: (g, 0, 0, i, 0)),
                  pl.BlockSpec(pk_blk, lambda g, i: (g, 0, 0, n - 1 - i, 0)), st],
        out_specs=[pl.BlockSpec(y_blk, lambda g, i: (g, 0, i, 0)),
                   pl.BlockSpec(y_blk, lambda g, i: (g, 0, n - 1 - i, 0)), st],
        out_shape=[y_shape, y_shape, jax.ShapeDtypeStruct((2, b * nh, HEAD * HEAD), F32)],
        scratch_shapes=[pltpu.VMEM((HEAD * HEAD, LANES), F32),
                        pltpu.VMEM((SCAN_T, 5 * HEAD + 2 * SUBLANES, LANES), F32),
                        pltpu.VMEM((SCAN_T, HEAD, LANES), F32),
                        pltpu.VMEM((2 * HEAD, LANES), F32), pltpu.VMEM((HEAD, LANES), F32)],
        compiler_params=_cparams("parallel", "arbitrary"),
        name="wkv_scan",
    )(pkf, pkb, s0)


def _rwkv_post_kernel(yf_ref, yb_ref, bonus_ref, g_ref, gng_ref, gnb_ref, o_ref):
    y = yf_ref[0, :, :, 0:HEAD] + yb_ref[0, :, :, 0:HEAD]
    mu = jnp.mean(y, axis=-1, keepdims=True)
    yc = y - mu
    var = jnp.mean(yc * yc, axis=-1, keepdims=True)
    yn = yc * lax.rsqrt(var + GN_EPS)
    yt = jnp.concatenate([yn[h] for h in range(yn.shape[0])], axis=-1)
    o_ref[0] = (yt * gng_ref[...] + gnb_ref[...] + bonus_ref[0]) * g_ref[0]


def _rwkv_post(yf, yb, bonus, g, gn_g, gn_b, tl):
    b, nh, l, _ = yf.shape
    w = nh * HEAD
    hm = pl.BlockSpec((1, nh, tl, LANES), lambda b_, i: (b_, 0, i, 0))
    tok = pl.BlockSpec((1, tl, w), lambda b_, i: (b_, i, 0))
    vec = pl.BlockSpec((1, w), lambda b_, i: (0, 0))
    return pl.pallas_call(
        _rwkv_post_kernel,
        grid=(b, l // tl),
        in_specs=[hm, hm, tok, tok, vec, vec],
        out_specs=tok,
        out_shape=jax.ShapeDtypeStruct((b, l, w), F32),
        compiler_params=_cparams("parallel", "parallel"),
        name="rwkv_post",
    )(yf, yb, bonus, g, gn_g, gn_b)


def _hy_prep_kernel(p_ref, pp_ref, pn_ref, sw_ref, x0_o, u_o, ub_o, *, seq_len, width):
    pc = _conv3(p_ref, pp_ref, pn_ref, sw_ref[...], seq_len)
    w = width
    x0_o[0] = pc[:, 0:w]
    u = pc[:, w:2 * w] * pc[:, 2 * w:3 * w]
    u_o[0] = u
    ub_o[0] = u.astype(BF16)


def _hy_prep(p_h, short_w, tl, seq_len):
    b, l, w3 = p_h.shape
    assert seq_len % tl == 0
    w = w3 // 3
    prev, nxt = _halo_specs(tl, w3, l)
    tok = pl.BlockSpec((1, tl, w), lambda b_, i: (b_, i, 0))
    return pl.pallas_call(
        functools.partial(_hy_prep_kernel, seq_len=seq_len, width=w),
        grid=(b, l // tl),
        in_specs=[pl.BlockSpec((1, tl, w3), lambda b_, i: (b_, i, 0)), prev, nxt,
                  pl.BlockSpec((3, w3), lambda b_, i: (0, 0))],
        out_specs=[tok, tok, tok],
        out_shape=[jax.ShapeDtypeStruct((b, l, w), F32), jax.ShapeDtypeStruct((b, l, w), F32),
                   jax.ShapeDtypeStruct((b, l, w), BF16)],
        compiler_params=_cparams("parallel", "parallel"),
        name="hy_prep",
    )(p_h, p_h, p_h, short_w)


def _hy_mlp_kernel(z_ref, w1_ref, b1_ref, f_ref, w2_ref, b2_ref, w3_ref, dec_ref, h_o):
    z = z_ref[...]
    t01 = z[:, 0:1]
    w1h, w1l = _split2(w1_ref[...])
    w2h, w2l = _split2(w2_ref[...])
    w3h, w3l = _split2(w3_ref[...])
    h = jnp.sin(f_ref[0:1, :] * (_dot3(z, w1h, w1l) + b1_ref[...]))
    h = jnp.sin(f_ref[1:2, :] * (_dot3(h, w2h, w2l) + b2_ref[...]))
    h = _dot3(h, w3h, w3l) * jnp.exp(-t01 * jnp.abs(dec_ref[...]))
    h_o[0] = h.astype(BF16)


def _hy_mlp(z, lp, tl):
    n = z.shape[0]
    c2 = lp["hy_w3"].shape[1]

    def full(a):
        return pl.BlockSpec(a.shape, lambda i: (0,) * a.ndim)

    consts = [lp["hy_w1"], lp["hy_b1"], lp["hy_freq"], lp["hy_w2"], lp["hy_b2"], lp["hy_w3"],
              lp["hy_decay"]]
    out = pl.BlockSpec((1, tl, c2), lambda i: (0, i, 0))
    return pl.pallas_call(
        _hy_mlp_kernel,
        grid=(n // tl,),
        in_specs=[pl.BlockSpec((tl, LANES), lambda i: (i, 0))] + [full(a) for a in consts],
        out_specs=out,
        out_shape=jax.ShapeDtypeStruct((1, n, c2), BF16),
        compiler_params=_cparams("parallel"),
        name="hy_mlp",
    )(z, *consts)


def _dft_filter_kernel(fr_ref, fi_ref, x_ref, hre_o, him_o):
    x = x_ref[0]
    w = x.shape[1] // 2
    xre = _dot(fr_ref[...], x)
    xim = _dot(fi_ref[...], x)
    hre_o[...] = xre[:, :w] + xre[:, w:]
    row = lax.broadcasted_iota(jnp.int32, (xre.shape[0], 1), 0) + pl.program_id(0) * xre.shape[0]
    him_o[...] = jnp.where(row == 0, xim[:, :w] + xim[:, w:], xim[:, :w] - xim[:, w:])


def _dft_filter(mats, h, tf):
    n = h.shape[1]
    w = h.shape[2] // 2
    ft = pl.BlockSpec((tf, n), lambda i: (i, 0))
    xs = pl.BlockSpec((1, n, 2 * w), lambda i: (0, 0, 0))
    out = pl.BlockSpec((tf, w), lambda i: (i, 0))
    return pl.pallas_call(
        _dft_filter_kernel,
        grid=(n // tf,),
        in_specs=[ft, ft, xs],
        out_specs=[out, out],
        out_shape=[jax.ShapeDtypeStruct((n, w), F32)] * 2,
        compiler_params=_cparams("parallel"),
        name="dft_filter",
    )(mats["fre"], mats["fim"], h)


def _dft_signal_kernel(fr_ref, fi_ref, x_ref, hre_ref, him_ref, pr_o, pi_o):
    x = x_ref[0]
    xre = _dot(fr_ref[...], x)
    xim = _dot(fi_ref[...], x)
    hre, him = hre_ref[...], him_ref[...]
    row = lax.broadcasted_iota(jnp.int32, (xre.shape[0], 1), 0) + pl.program_id(1) * xre.shape[0]
    pr_o[0] = jnp.where(row == 0, xre * hre, xre * hre - xim * him).astype(BF16)
    pi_o[0] = jnp.where(row == 0, xim * him, xre * him + xim * hre).astype(BF16)


def _dft_signal(mats, ub, hre, him, tf):
    b, n, w = ub.shape
    ft = pl.BlockSpec((tf, n), lambda b_, i: (i, 0))
    xs = pl.BlockSpec((1, n, w), lambda b_, i: (b_, 0, 0))
    hs = pl.BlockSpec((tf, w), lambda b_, i: (i, 0))
    out = pl.BlockSpec((1, tf, w), lambda b_, i: (b_, i, 0))
    return pl.pallas_call(
        _dft_signal_kernel,
        grid=(b, n // tf),
        in_specs=[ft, ft, xs, hs, hs],
        out_specs=[out] * 2,
        out_shape=[jax.ShapeDtypeStruct((b, n, w), BF16)] * 2,
        compiler_params=_cparams("parallel", "parallel"),
        name="dft_signal",
    )(mats["fre"], mats["fim"], ub, hre, him)


def _dft_inverse_kernel(gr_ref, gi_ref, pr_ref, pi_ref, x0_ref, u_ref, bias_ref, o_ref):
    y = _dot(gr_ref[...], pr_ref[0]) + _dot(gi_ref[...], pi_ref[0])
    o_ref[0] = x0_ref[0] * (y + u_ref[0] * bias_ref[...])


def _dft_inverse(mats, pr, pi, x0, u, bias, tt):
    b, n, w = u.shape
    gt = pl.BlockSpec((tt, n), lambda b_, i: (i, 0))
    ps = pl.BlockSpec((1, n, w), lambda b_, i: (b_, 0, 0))
    tok = pl.BlockSpec((1, tt, w), lambda b_, i: (b_, i, 0))
    return pl.pallas_call(
        _dft_inverse_kernel,
        grid=(b, n // tt),
        in_specs=[gt, gt, ps, ps, tok, tok, pl.BlockSpec((1, w), lambda b_, i: (0, 0))],
        out_specs=tok,
        out_shape=jax.ShapeDtypeStruct((b, n, w), F32),
        compiler_params=_cparams("parallel", "parallel"),
        name="dft_inverse",
    )(mats["gre"], mats["gim"], pr, pi, x0, u, bias)


def _swap_halves(x):
    lane = lax.broadcasted_iota(jnp.int32, x.shape, 1)
    half = HEAD // 2
    return jnp.where(lane % HEAD < half, pltpu.roll(x, LANES - half, 1), pltpu.roll(x, half, 1))


def _attn_prep_kernel(*refs, rope, wq, wkv):
    if rope:
        p_ref, qn_ref, kn_ref, ones_ref, cos_ref, sin_ref, q_o, k_o, v_o, kt_o = refs
    else:
        p_ref, qn_ref, kn_ref, ones_ref, q_o, k_o, v_o, kt_o = refs
    ones_bd = ones_ref[...]

    def norm_rope(x, gain):
        ms = _segsum(x * x, ones_bd) * (1.0 / HEAD)
        xn = x * lax.rsqrt(ms + QK_EPS) * gain
        if rope:
            xn = xn * cos_ref[...] + _swap_halves(xn) * sin_ref[...]
        return xn

    scale = HEAD ** -0.5
    for c in range(wq // LANES):
        xq = norm_rope(p_ref[0, :, c * LANES:(c + 1) * LANES], qn_ref[...]) * scale
        q_o[0, 2 * c] = xq[:, :HEAD].astype(BF16)
        q_o[0, 2 * c + 1] = xq[:, HEAD:].astype(BF16)
    for c in range(wkv // LANES):
        x = p_ref[0, :, wq + c * LANES:wq + (c + 1) * LANES]
        ms = _segsum(x * x, ones_bd) * (1.0 / HEAD)
        xk = x * lax.rsqrt(ms + QK_EPS) * kn_ref[...]
        kt_o[0, :, c * LANES:(c + 1) * LANES] = xk
        if rope:
            xk = xk * cos_ref[...] + _swap_halves(xk) * sin_ref[...]
        k_o[0, 2 * c] = xk[:, :HEAD].astype(BF16)
        k_o[0, 2 * c + 1] = xk[:, HEAD:].astype(BF16)
        xv = p_ref[0, :, wq + wkv + c * LANES:wq + wkv + (c + 1) * LANES]
        v_o[0, 2 * c] = xv[:, :HEAD].astype(BF16)
        v_o[0, 2 * c + 1] = xv[:, HEAD:].astype(BF16)


def _attn_prep(p_a, lp, rope_tabs, tl, wq, wkv):
    b, l, wa = p_a.shape
    nq, nkv = wq // HEAD, wkv // HEAD
    rope = rope_tabs is not None
    vec = pl.BlockSpec((1, LANES), lambda b_, i: (0, 0))
    in_specs = [pl.BlockSpec((1, tl, wa), lambda b_, i: (b_, i, 0)), vec, vec,
                pl.BlockSpec((LANES, LANES), lambda b_, i: (0, 0))]
    args = [p_a, lp["attn_qn"], lp["attn_kn"], lp["ones_pair"]]
    if rope:
        tab = pl.BlockSpec((tl, LANES), lambda b_, i: (i, 0))
        in_specs += [tab, tab]
        args += list(rope_tabs)
    return pl.pallas_call(
        functools.partial(_attn_prep_kernel, rope=rope, wq=wq, wkv=wkv),
        grid=(b, l // tl),
        in_specs=in_specs,
        out_specs=[pl.BlockSpec((1, nq, tl, HEAD), lambda b_, i: (b_, 0, i, 0)),
                   pl.BlockSpec((1, nkv, tl, HEAD), lambda b_, i: (b_, 0, i, 0)),
                   pl.BlockSpec((1, nkv, tl, HEAD), lambda b_, i: (b_, 0, i, 0)),
                   pl.BlockSpec((1, tl, wkv), lambda b_, i: (b_, i, 0))],
        out_shape=[jax.ShapeDtypeStruct((b, nq, l, HEAD), BF16),
                   jax.ShapeDtypeStruct((b, nkv, l, HEAD), BF16),
                   jax.ShapeDtypeStruct((b, nkv, l, HEAD), BF16),
                   jax.ShapeDtypeStruct((b, l, wkv), F32)],
        compiler_params=_cparams("parallel", "parallel"),
        name="attn_prep",
    )(*args)


def _attn_kernel(q_ref, k_ref, v_ref, o_ref):
    k, v = k_ref[0, 0], v_ref[0, 0]
    outs = []
    for h in range(q_ref.shape[1]):
        s = lax.dot_general(q_ref[0, h], k, (((1,), (1,)), ((), ())), preferred_element_type=F32)
        m = jnp.max(s, axis=-1, keepdims=True)
        p = jnp.exp(s - m)
        den = jnp.sum(p, axis=-1, keepdims=True)
        outs.append(_dot(p.astype(BF16), v) / den)
    o_ref[0] = jnp.concatenate(outs, axis=-1)


def _attention(q, k, v, tq):
    b, nq, l, _ = q.shape
    nkv, lk = k.shape[1], k.shape[2]
    g = nq // nkv
    kv = pl.BlockSpec((1, 1, lk, HEAD), lambda b_, h, i: (b_, h, 0, 0))
    return pl.pallas_call(
        _attn_kernel,
        grid=(b, nkv, l // tq),
        in_specs=[pl.BlockSpec((1, g, tq, HEAD), lambda b_, h, i: (b_, h, i, 0)), kv, kv],
        out_specs=pl.BlockSpec((1, tq, g * HEAD), lambda b_, h, i: (b_, i, h)),
        out_shape=jax.ShapeDtypeStruct((b, l, nq * HEAD), F32),
        compiler_params=_cparams("parallel", "parallel", "parallel"),
        name="attention",
    )(q, k, v)


def _out_proj_kernel(yr_ref, yh_ref, ya_ref, x_ref, g_ref, w_ref, lng_ref, lnb_ref, o_ref, *, alpha):
    wr, wh = yr_ref.shape[2], yh_ref.shape[2]
    mix = _dot(yr_ref[0].astype(BF16), w_ref[0:wr, :])
    mix = mix + _dot(yh_ref[0].astype(BF16), w_ref[wr:wr + wh, :])
    mix = mix + _dot(ya_ref[0].astype(BF16), w_ref[wr + wh:, :])
    o_ref[0] = _layer_norm(alpha * x_ref[0] + g_ref[0] * mix, lng_ref[...], lnb_ref[...])


def _out_proj(y_r, y_h, y_a, x, gate, w_out, ln_g, ln_b, tl, alpha):
    bx, lx, d = x.shape
    sel = _bm(gate)

    def tok(a):
        return pl.BlockSpec((1, tl, a.shape[2]), lambda b, i: (b, i, 0))

    vec = pl.BlockSpec((1, d), lambda b, i: (0, 0))
    return pl.pallas_call(
        functools.partial(_out_proj_kernel, alpha=alpha),
        grid=(bx, lx // tl),
        in_specs=[tok(y_r), tok(y_h), tok(y_a), tok(x),
                  pl.BlockSpec((1, 1, d), lambda b, i: (sel(b), 0, 0)),
                  pl.BlockSpec(w_out.shape, lambda b, i: (0, 0)), vec, vec],
        out_specs=tok(x),
        out_shape=jax.ShapeDtypeStruct((bx, lx, d), F32),
        compiler_params=_cparams("parallel", "parallel"),
        name="out_proj",
    )(y_r, y_h, y_a, x, gate, w_out, ln_g, ln_b)


def _ffn_kernel(x_ref, xp_ref, xn_ref, sc_ref, sh_ref, g_ref, wa_ref, wb_ref, ca_ref, cb_ref,
                wd_ref, lng_ref, lnb_ref, o_ref, h_s, acc_s, ua_s, ub_s, *, seq_len, alpha):
    tl = x_ref.shape[1]
    i = pl.program_id(1)
    j = pl.program_id(2)
    halo = FFN_HALO

    @pl.when(j == 0)
    def _():
        sc, sh = 1.0 + sc_ref[0], sh_ref[0]
        h_s[0:halo] = (xp_ref[0] * sc + sh).astype(BF16)
        h_s[halo:halo + tl] = (x_ref[0] * sc + sh).astype(BF16)
        h_s[halo + tl:] = (xn_ref[0] * sc + sh).astype(BF16)
        acc_s[...] = jnp.zeros_like(acc_s)

    row = lax.broadcasted_iota(jnp.int32, (tl, 1), 0)
    pos = (i * tl + row) % seq_len
    first = pos == 0
    last = pos == seq_len - 1
    h = h_s[...]

    ua_s[...] = _dot(h, wa_ref[...])
    ub_s[...] = _dot(h, wb_ref[...])

    def conv(u_s, c_ref):
        um = jnp.where(first, 0.0, u_s[halo - 1:halo - 1 + tl, :])
        up = jnp.where(last, 0.0, u_s[halo + 1:halo + 1 + tl, :])
        return um * c_ref[0:1, :] + u_s[halo:halo + tl, :] * c_ref[1:2, :] + up * c_ref[2:3, :]

    a = conv(ua_s, ca_ref)
    b = conv(ub_s, cb_ref)
    f = (a * _sigmoid(a) * b).astype(BF16)
    acc_s[...] += _dot(f, wd_ref[...])

    @pl.when(j == pl.num_programs(2) - 1)
    def _():
        o_ref[0] = _layer_norm(alpha * x_ref[0] + g_ref[0] * acc_s[...], lng_ref[...], lnb_ref[...])


def _ffn(x, sc, sh, gate, w_up, conv_w, w_down, ln_g, ln_b, tl, tn, seq_len, alpha):
    bx, lx, d = x.shape
    dff = w_down.shape[0]
    nj = dff // tn
    sel = _bm(sc)
    prev, nxt = _halo_specs(tl, d, lx, FFN_HALO)
    prev3 = pl.BlockSpec(prev.block_shape, lambda b, i, j: prev.index_map(b, i))
    nxt3 = pl.BlockSpec(nxt.block_shape, lambda b, i, j: nxt.index_map(b, i))
    mod = pl.BlockSpec((1, 1, d), lambda b, i, j: (sel(b), 0, 0))
    vec = pl.BlockSpec((1, d), lambda b, i, j: (0, 0))
    tok = pl.BlockSpec((1, tl, d), lambda b, i, j: (b, i, 0))
    return pl.pallas_call(
        functools.partial(_ffn_kernel, seq_len=seq_len, alpha=alpha),
        grid=(bx, lx // tl, nj),
        in_specs=[tok, prev3, nxt3, mod, mod, mod,
                  pl.BlockSpec((d, tn), lambda b, i, j: (0, j)),
                  pl.BlockSpec((d, tn), lambda b, i, j: (0, j + nj)),
                  pl.BlockSpec((3, tn), lambda b, i, j: (0, j)),
                  pl.BlockSpec((3, tn), lambda b, i, j: (0, j + nj)),
                  pl.BlockSpec((tn, d), lambda b, i, j: (j, 0)), vec, vec],
        out_specs=tok,
        out_shape=jax.ShapeDtypeStruct((bx, lx, d), F32),
        scratch_shapes=[pltpu.VMEM((tl + 2 * FFN_HALO, d), BF16), pltpu.VMEM((tl, d), F32),
                        pltpu.VMEM((tl + 2 * FFN_HALO, tn), F32),
                        pltpu.VMEM((tl + 2 * FFN_HALO, tn), F32)],
        compiler_params=_cparams("parallel", "parallel", "arbitrary"),
        name="conv_ffn",
    )(x, x, x, sc, sh, gate, w_up, w_up, conv_w, conv_w, w_down, ln_g, ln_b)


def _split2_host(x):
    hi = lax.bitcast_convert_type(lax.bitcast_convert_type(x, jnp.uint32) & jnp.uint32(0xFFFF0000), F32)
    return hi.astype(BF16), (x - hi).astype(BF16)


def _dft_mats(n):
    big = 2 * n
    lo = min(n, 64)
    k = jnp.arange(n, dtype=jnp.int32)[:, None]

    def table(t):
        ang = ((k * t[None, :]) % big).astype(F32) * (2.0 * math.pi / big)
        return jnp.cos(ang), jnp.sin(ang)

    ca, sa = table(lo * jnp.arange(n // lo, dtype=jnp.int32))
    cb, sb = table(jnp.arange(lo, dtype=jnp.int32))
    cos = (ca[:, :, None] * cb[:, None, :] - sa[:, :, None] * sb[:, None, :]).reshape(n, n)
    sin = (sa[:, :, None] * cb[:, None, :] + ca[:, :, None] * sb[:, None, :]).reshape(n, n)
    t = jnp.arange(n, dtype=jnp.int32)[None, :]
    fre = cos
    fim = jnp.where(k == 0, jnp.where(t % 2 == 0, 1.0, -1.0), -sin)
    scale = jnp.where(k == 0, 1.0 / big, 2.0 / big)
    return {"fre": fre.astype(BF16), "fim": fim.astype(BF16),
            "gre": (fre * scale).T.astype(BF16), "gim": (fim * scale).T.astype(BF16)}


def _hyena_features(n, n_bands):
    t01 = jnp.linspace(0.0, 1.0, n, dtype=F32)[:, None]
    pos = jnp.arange(n, dtype=F32)[:, None]
    bands = jnp.linspace(1e-4, n_bands - 1, n_bands, dtype=F32)[None, :]
    ang = (2.0 * math.pi / n) * pos * bands
    z = jnp.concatenate([t01, jnp.cos(ang), -jnp.sin(ang)], -1)
    return jnp.pad(z, ((0, 0), (0, LANES - z.shape[1])))


def _rope_tables(n_tokens):
    rows = n_tokens // GRID_W
    row = jnp.repeat(jnp.arange(rows, dtype=F32), GRID_W)
    col = jnp.tile(jnp.arange(GRID_W, dtype=F32), rows)
    n_freq = HEAD // 4
    inv = ROPE_THETA ** (-jnp.arange(n_freq, dtype=F32) / n_freq)
    ang = jnp.concatenate([row[:, None] * inv, col[:, None] * inv], -1)
    cos, sin = jnp.cos(ang), jnp.sin(ang)
    cos2 = jnp.tile(jnp.concatenate([cos, cos], -1), (1, LANES // HEAD))
    sin2 = jnp.tile(jnp.concatenate([-sin, sin], -1), (1, LANES // HEAD))
    return cos2, sin2


def _block_ones(n, group):
    idx = jnp.arange(n) // group
    return (idx[:, None] == idx[None, :]).astype(BF16)


def _pad_to(a, shape):
    return jnp.pad(a, [(0, s - d) for d, s in zip(a.shape, shape)])


def _block_diag2(a, b):
    za = jnp.zeros((a.shape[0], b.shape[1]), a.dtype)
    zb = jnp.zeros((b.shape[0], a.shape[1]), a.dtype)
    return jnp.concatenate([jnp.concatenate([a, za], 1), jnp.concatenate([zb, b], 1)], 0)


def _layer_params(l, P, dims):
    w_r, w_h, wq, wkv = dims["w_rwkv"], dims["w_hyena"], dims["wq"], dims["wkv"]
    rwkv_cols = 3 * w_r + 2 * P["rwkv_w2"].shape[2] + 2 * P["rwkv_a2"].shape[2] + P["rwkv_g2"].shape[1]
    rwkv_pad = -(-rwkv_cols // (2 * LANES)) * (2 * LANES)
    hy_cols = 3 * w_h
    w_in = P["w_in"][l]
    lp = {"rwkv_w": w_r}
    lp["w_in_r"] = _pad_to(w_in[:, :rwkv_cols], (w_in.shape[0], rwkv_pad)).astype(BF16)
    lp["w_in_h"] = w_in[:, rwkv_cols:rwkv_cols + hy_cols].astype(BF16)
    lp["w_in_a"] = w_in[:, rwkv_cols + hy_cols:].astype(BF16)
    lp["rwkv_shift"] = _pad_to(P["rwkv_shift"][l], (3, rwkv_pad))
    lp["rwkv_w0"] = P["rwkv_w0"][l]
    lp["rwkv_a0"] = P["rwkv_a0"][l]
    for nm in ("rwkv_kk", "rwkv_ka", "rwkv_rk", "rwkv_gn_g", "rwkv_gn_b", "hy_bias",
               "ln1_g", "ln1_b", "ln2_g", "ln2_b"):
        lp[nm] = P[nm][l][None, :]
    lp["w2_hi"], lp["w2_lo"] = _split2_host(_block_diag2(P["rwkv_w2"][l, 0], P["rwkv_w2"][l, 1]))
    lp["a2_hi"], lp["a2_lo"] = _split2_host(_block_diag2(P["rwkv_a2"][l, 0], P["rwkv_a2"][l, 1]))
    lp["g2_hi"], lp["g2_lo"] = _split2_host(P["rwkv_g2"][l])
    lp["ones_head"] = _block_ones(w_r, HEAD)
    lp["ones_pair"] = _block_ones(LANES, HEAD)
    lp["hy_short"] = P["hy_short"][l]
    ffn_w = P["hy_w1"].shape[2]
    lp["hy_w1"] = _pad_to(P["hy_w1"][l], (LANES, LANES))
    lp["hy_b1"] = _pad_to(P["hy_b1"][l][None, :], (1, LANES))
    lp["hy_freq"] = _pad_to(P["hy_freq"][l], (2, LANES))
    lp["hy_w2"] = _pad_to(P["hy_w2"][l], (LANES, LANES))
    lp["hy_b2"] = _pad_to(P["hy_b2"][l][None, :], (1, LANES))
    lp["hy_w3"] = _pad_to(P["hy_w3"][l], (LANES, 2 * w_h))
    lp["hy_decay"] = P["hy_decay"][l].reshape(1, 2 * w_h)
    del ffn_w
    lp["attn_qn"] = jnp.tile(P["attn_qn"][l], LANES // HEAD)[None, :]
    lp["attn_kn"] = jnp.tile(P["attn_kn"][l], LANES // HEAD)[None, :]
    lp["w_out"] = P["w_out"][l].astype(BF16)
    lp["ffn_up"] = P["ffn_up"][l].astype(BF16)
    lp["ffn_conv"] = P["ffn_conv"][l]
    lp["ffn_down"] = P["ffn_down"][l].astype(BF16)
    return lp


def _trunk_layer(x, mod6, lp, dims, rope_tabs, ctx_kv, s0, dft, z_feat, tiles):
    b, l, d = x.shape
    sh1, sc1, g1, sh2, sc2, g2 = mod6
    shared = sh1.shape[0] == 1
    alpha = dims["alpha"]
    w_r, w_h, wq, wkv = dims["w_rwkv"], dims["w_hyena"], dims["wq"], dims["wkv"]
    nh = w_r // HEAD
    tl_mm, tl_ew = tiles["mm"], tiles["ew"]

    xm = x.reshape(1, b * l, d) if shared else x
    p_r = _mod_proj(xm, sc1, sh1, lp["w_in_r"], tl_mm).reshape(b, l, -1)
    p_h = _mod_proj(xm, sc1, sh1, lp["w_in_h"], tl_mm).reshape(b, l, -1)
    p_a = _mod_proj(xm, sc1, sh1, lp["w_in_a"], tl_mm).reshape(b, l, -1)

    pkf, pkb, bonus, gate = _rwkv_prep(p_r, lp, tl_ew, l)
    yf, yb, s_fin = _wkv_scan(pkf, pkb, s0)
    y_r = _rwkv_post(yf, yb, bonus, gate, lp["rwkv_gn_g"], lp["rwkv_gn_b"], tl_ew)

    x0, u, ub = _hy_prep(p_h, lp["hy_short"], tl_ew, l)
    tf = min(l, 256)
    hre, him = _dft_filter(dft, _hy_mlp(z_feat, lp, tf), tf)
    pr, pi = _dft_signal(dft, ub, hre, him, tf)
    y_h = _dft_inverse(dft, pr, pi, x0, u, lp["hy_bias"], tf)

    q, k, vv, k_tok = _attn_prep(p_a, lp, rope_tabs, tl_ew, wq, wkv)
    if ctx_kv is not None:
        ck, cv = ctx_kv
        k = jnp.concatenate([k, jnp.swapaxes(ck, 1, 2).astype(BF16)], axis=2)
        vv = jnp.concatenate([vv, jnp.swapaxes(cv, 1, 2).astype(BF16)], axis=2)
    y_a = _attention(q, k, vv, tiles["tq"])

    def m(a):
        return a.reshape(1, b * l, a.shape[-1]) if shared else a

    x1 = _out_proj(m(y_r), m(y_h), m(y_a), xm, g1, lp["w_out"], lp["ln1_g"], lp["ln1_b"],
                   tl_mm, alpha)
    x2 = _ffn(x1, sc2, sh2, g2, lp["ffn_up"], lp["ffn_conv"], lp["ffn_down"],
              lp["ln2_g"], lp["ln2_b"], tl_mm, tiles["ffn_tn"], l, alpha)
    v_tok = p_a[..., wq + wkv:]
    return x2.reshape(b, l, d), k_tok, v_tok, s_fin


def kernel(x_prompt, x_sample, cache_k, cache_v, state_rwkv, c, c_ctx, w_mod, b_mod, w_in, rwkv_shift, rwkv_w0, rwkv_w2, rwkv_a0, rwkv_a2, rwkv_kk, rwkv_ka, rwkv_rk, rwkv_g2, rwkv_gn_g, rwkv_gn_b, hy_short, hy_w1, hy_b1, hy_freq, hy_w2, hy_b2, hy_w3, hy_decay, hy_bias, attn_qn, attn_kn, w_out, ln1_g, ln1_b, ln2_g, ln2_b, ffn_up, ffn_conv, ffn_down):
    P = dict(w_in=w_in, rwkv_shift=rwkv_shift, rwkv_w0=rwkv_w0, rwkv_w2=rwkv_w2, rwkv_a0=rwkv_a0,
             rwkv_a2=rwkv_a2, rwkv_kk=rwkv_kk, rwkv_ka=rwkv_ka, rwkv_rk=rwkv_rk, rwkv_g2=rwkv_g2,
             rwkv_gn_g=rwkv_gn_g, rwkv_gn_b=rwkv_gn_b, hy_short=hy_short, hy_w1=hy_w1, hy_b1=hy_b1,
             hy_freq=hy_freq, hy_w2=hy_w2, hy_b2=hy_b2, hy_w3=hy_w3, hy_decay=hy_decay,
             hy_bias=hy_bias, attn_qn=attn_qn, attn_kn=attn_kn, w_out=w_out, ln1_g=ln1_g,
             ln1_b=ln1_b, ln2_g=ln2_g, ln2_b=ln2_b, ffn_up=ffn_up, ffn_conv=ffn_conv,
             ffn_down=ffn_down)
    depth, d = w_mod.shape[0], w_mod.shape[1]
    bc, lc, _ = x_prompt.shape
    bd, ld, _ = x_sample.shape
    nkv = cache_k.shape[3]
    nh = state_rwkv.shape[3]
    w_r = nh * HEAD
    w_h = hy_bias.shape[1]
    wkv = nkv * HEAD
    wq = w_in.shape[2] - (3 * w_r + 2 * rwkv_w2.shape[2] + 2 * rwkv_a2.shape[2] + rwkv_g2.shape[1]) \
        - 3 * w_h - 2 * wkv
    dims = dict(w_rwkv=w_r, w_hyena=w_h, wq=wq, wkv=wkv, alpha=(2 * depth) ** 0.25)

    rows = -(-(bd + 1) // SUBLANES) * SUBLANES
    cond = _pad_to(jnp.concatenate([c, c_ctx[None, :]], 0), (rows, d))
    mod = _modulation(cond, w_mod, b_mod)

    rope_tabs = _rope_tables(ld)
    dft_c, dft_d = _dft_mats(lc), _dft_mats(ld)
    n_bands = (hy_w1.shape[1] - 1) // 2
    z_c, z_d = _hyena_features(lc, n_bands), _hyena_features(ld, n_bands)
    tiles_c = dict(mm=min(512, bc * lc), ew=min(256, lc), tq=min(256, lc), ffn_tn=512)
    tiles_d = dict(mm=min(512, ld), ew=min(256, ld), tq=min(512, ld), ffn_tn=512)

    xp, xs = x_prompt, x_sample
    zero_state = jnp.zeros((2, bc * nh, HEAD * HEAD), F32)
    new_k, new_v, new_s = [], [], []
    for l in range(depth):
        lp = _layer_params(l, P, dims)
        mod_d = [m[:bd, None, :] for m in jnp.split(mod[l], 6, axis=-1)]
        mod_c = [m[bd:bd + 1, None, :] for m in jnp.split(mod[l], 6, axis=-1)]
        xp, k_c, v_c, s_c = _trunk_layer(xp, mod_c, lp, dims, None, None, zero_state, dft_c, z_c, tiles_c)
        new_k.append(k_c.reshape(bc, lc, nkv, HEAD))
        new_v.append(v_c.reshape(bc, lc, nkv, HEAD))
        new_s.append(jnp.transpose(s_c.reshape(2, bc, nh, HEAD, HEAD), (1, 0, 2, 4, 3)))
        s0 = jnp.transpose(state_rwkv[:, l], (1, 0, 2, 4, 3)).reshape(2, bd * nh, HEAD * HEAD)
        xs, _, _, _ = _trunk_layer(xs, mod_d, lp, dims, rope_tabs, (cache_k[:, l], cache_v[:, l]),
                                   s0, dft_d, z_d, tiles_d)
    return (xp, xs, jnp.stack(new_k, axis=1), jnp.stack(new_v, axis=1), jnp.stack(new_s, axis=1))
```

```python
import functools
import math

import jax
import jax.numpy as jnp
from jax import lax
from jax.experimental import pallas as pl
from jax.experimental.pallas import tpu as pltpu

F32 = jnp.float32
BF16 = jnp.bfloat16

HEAD = 64
LANES = 128
SUBLANES = 8
VMEM_LIMIT = 52 * 1024 * 1024
LN_EPS = 1e-5
QK_EPS = 1e-6
GN_EPS = 64e-5
ROPE_THETA = 10000.0
GRID_W = 64
SCAN_T = 16
WKV_TILES = 3
FFN_HALO = 16


def _cparams(*sem):
    return pltpu.CompilerParams(dimension_semantics=sem, vmem_limit_bytes=VMEM_LIMIT)


def _dot(a, b):
    return jnp.dot(a, b, preferred_element_type=F32)


def _hi_f32(x):
    u = pltpu.bitcast(x, jnp.uint32) & jnp.uint32(0xFFFF0000)
    return pltpu.bitcast(u, F32)


def _split2(x):
    h = _hi_f32(x)
    return h.astype(BF16), (x - h).astype(BF16)


def _dot3(a, b_hi, b_lo):
    a_hi, a_lo = _split2(a)
    return _dot(a_hi, b_hi) + (_dot(a_lo, b_hi) + _dot(a_hi, b_lo))


def _segsum(x, ones_bd):
    hi, lo = _split2(x)
    return _dot(hi, ones_bd) + _dot(lo, ones_bd)


def _layer_norm(x, g, b):
    mu = jnp.mean(x, axis=-1, keepdims=True)
    xc = x - mu
    var = jnp.mean(xc * xc, axis=-1, keepdims=True)
    return xc * lax.rsqrt(var + LN_EPS) * g + b


def _sigmoid(x):
    return 1.0 / (1.0 + jnp.exp(-x))


def _conv3(x_ref, prev_ref, next_ref, w, seq_len):
    x = x_ref[0]
    tl = x.shape[0]
    i = pl.program_id(1)
    prev_row = jnp.where((i * tl) % seq_len == 0, 0.0, prev_ref[0, SUBLANES - 1:SUBLANES, :])
    next_row = jnp.where(((i + 1) * tl) % seq_len == 0, 0.0, next_ref[0, 0:1, :])
    row = lax.broadcasted_iota(jnp.int32, (SUBLANES, 1), 0)
    xm = pltpu.roll(x, 1, 0)
    xp = pltpu.roll(x, tl - 1, 0)
    xm = jnp.concatenate([jnp.where(row == 0, prev_row, xm[:SUBLANES]), xm[SUBLANES:]], axis=0)
    xp = jnp.concatenate([xp[:tl - SUBLANES],
                          jnp.where(row == SUBLANES - 1, next_row, xp[tl - SUBLANES:])], axis=0)
    return xm * w[0:1, :] + x * w[1:2, :] + xp * w[2:3, :]


def _halo_specs(tl, width, n_rows, halo=SUBLANES):
    r = tl // halo
    last = n_rows // halo - 1
    prev = pl.BlockSpec((1, halo, width), lambda b, i: (b, jnp.maximum(i * r - 1, 0), 0))
    nxt = pl.BlockSpec((1, halo, width), lambda b, i: (b, jnp.minimum((i + 1) * r, last), 0))
    return prev, nxt


def _bm(arr):
    if arr.shape[0] == 1:
        return lambda b: 0
    return lambda b: b


def _mod_kernel(c_ref, w_ref, b_ref, o_ref):
    c = c_ref[...]
    s = c * _sigmoid(c)
    w_hi, w_lo = _split2(w_ref[0])
    o_ref[0] = _dot3(s, w_hi, w_lo) + b_ref[0]


def _modulation(cond, w_mod, b_mod):
    depth, d, n = w_mod.shape
    rows = cond.shape[0]
    tn = 1024
    return pl.pallas_call(
        _mod_kernel,
        grid=(depth, n // tn),
        in_specs=[
            pl.BlockSpec((rows, d), lambda l, j: (0, 0)),
            pl.BlockSpec((1, d, tn), lambda l, j: (l, 0, j)),
            pl.BlockSpec((1, 1, tn), lambda l, j: (l, 0, j)),
        ],
        out_specs=pl.BlockSpec((1, rows, tn), lambda l, j: (l, 0, j)),
        out_shape=jax.ShapeDtypeStruct((depth, rows, n), F32),
        compiler_params=_cparams("parallel", "parallel"),
        name="modulation",
    )(cond, w_mod, b_mod.reshape(depth, 1, n))


def _proj_kernel(x_ref, sc_ref, sh_ref, w_ref, o_ref):
    h = (x_ref[0] * (1.0 + sc_ref[0]) + sh_ref[0]).astype(BF16)
    o_ref[0] = _dot(h, w_ref[...])


def _mod_proj(x, sc, sh, w, tl):
    bx, lx, d = x.shape
    n = w.shape[1]
    sel = _bm(sc)
    return pl.pallas_call(
        _proj_kernel,
        grid=(bx, lx // tl),
        in_specs=[
            pl.BlockSpec((1, tl, d), lambda b, i: (b, i, 0)),
            pl.BlockSpec((1, 1, d), lambda b, i: (sel(b), 0, 0)),
            pl.BlockSpec((1, 1, d), lambda b, i: (sel(b), 0, 0)),
            pl.BlockSpec((d, n), lambda b, i: (0, 0)),
        ],
        out_specs=pl.BlockSpec((1, tl, n), lambda b, i: (b, i, 0)),
        out_shape=jax.ShapeDtypeStruct((bx, lx, n), F32),
        compiler_params=_cparams("parallel", "parallel"),
        name="mod_proj",
    )(x, sc, sh, w)


def _store_packed(o_ref, pairs):
    for q, (xa, xb) in enumerate(pairs):
        for h in range(xa.shape[1] // HEAD):
            sl = slice(HEAD * h, HEAD * (h + 1))
            o_ref[0, h, q] = jnp.concatenate([xa[:, sl], xb[:, sl]], axis=-1)


def _rwkv_prep_kernel(p_ref, pp_ref, pn_ref, sw_ref, w0_ref, a0_ref, kkp_ref, ka_ref, rk_ref,
                      w2h_ref, w2l_ref, a2h_ref, a2l_ref, g2h_ref, g2l_ref, ones_ref,
                      pkf_o, pkb_o, bonus_o, g_o, *, seq_len, width):
    pc = _conv3(p_ref, pp_ref, pn_ref, sw_ref[...], seq_len)
    w = width
    r = pc[:, 0:w]
    k = pc[:, w:2 * w]
    v = pc[:, 2 * w:3 * w]
    wd = pc[:, 3 * w:3 * w + LANES]
    ad = pc[:, 3 * w + LANES:3 * w + 2 * LANES]
    gd = pc[:, 3 * w + 2 * LANES:3 * w + 3 * LANES]
    ones_bd = ones_ref[...]

    g_o[0] = _dot3(_sigmoid(gd), g2h_ref[...], g2l_ref[...])
    kkr = k * kkp_ref[...]
    kk = kkr * lax.rsqrt(_segsum(kkr * kkr, ones_bd) + 1e-12)
    lw = _dot3(jnp.tanh(wd), w2h_ref[...], w2l_ref[...])
    la = _dot3(ad, a2h_ref[...], a2l_ref[...])

    rrk = r * rk_ref[...]
    bonus = jnp.zeros_like(r)
    for d, pk_o in enumerate((pkf_o, pkb_o)):
        z = w0_ref[d:d + 1, :] + lw[:, d * w:(d + 1) * w]
        decay = jnp.exp(-math.exp(-0.5) * _sigmoid(z))
        a = _sigmoid(a0_ref[d:d + 1, :] + la[:, d * w:(d + 1) * w])
        kd = k * (1.0 + (a - 1.0) * ka_ref[...])
        _store_packed(pk_o, ((r, v), (kk, decay), (kd, kk * a)))
        bonus = bonus + _segsum(rrk * kd, ones_bd) * v
    bonus_o[0] = bonus


def _rwkv_prep(p_r, lp, tl, seq_len):
    b, l, wp = p_r.shape
    assert seq_len % tl == 0
    w = lp["rwkv_w"]
    nh = w // HEAD
    hm = jax.ShapeDtypeStruct((b, nh, WKV_TILES, l, LANES), F32)
    tok = jax.ShapeDtypeStruct((b, l, w), F32)
    prev, nxt = _halo_specs(tl, wp, l)

    def full(a):
        return pl.BlockSpec(a.shape, lambda b_, i: (0,) * a.ndim)

    consts = [lp["rwkv_shift"], lp["rwkv_w0"], lp["rwkv_a0"], lp["rwkv_kk"], lp["rwkv_ka"],
              lp["rwkv_rk"], lp["w2_hi"], lp["w2_lo"], lp["a2_hi"], lp["a2_lo"],
              lp["g2_hi"], lp["g2_lo"], lp["ones_head"]]
    hm_spec = pl.BlockSpec((1, nh, WKV_TILES, tl, LANES), lambda b_, i: (b_, 0, 0, i, 0))
    tok_spec = pl.BlockSpec((1, tl, w), lambda b_, i: (b_, i, 0))
    return pl.pallas_call(
        functools.partial(_rwkv_prep_kernel, seq_len=seq_len, width=w),
        grid=(b, l // tl),
        in_specs=[pl.BlockSpec((1, tl, wp), lambda b_, i: (b_, i, 0)), prev, nxt]
        + [full(a) for a in consts],
        out_specs=[hm_spec] * 2 + [tok_spec] * 2,
        out_shape=[hm] * 2 + [tok] * 2,
        compiler_params=_cparams("parallel", "parallel"),
        name="rwkv_prep",
    )(p_r, p_r, p_r, *consts)


def _allsum8(x):
    s = x[0:8]
    for q in range(1, x.shape[0] // SUBLANES):
        s = s + x[q * SUBLANES:(q + 1) * SUBLANES]
    s = s + pltpu.roll(s, 4, 0)
    s = s + pltpu.roll(s, 2, 0)
    return s + pltpu.roll(s, 1, 0)


def _wkv_kernel(pf_ref, pb_ref, s0_ref, yf_ref, yb_ref, so_ref, S, OPS, YS, ACC, GAM):
    i = pl.program_id(1)
    n = pl.num_programs(1)
    t_steps = pf_ref.shape[3]
    stride = WKV_TILES * t_steps
    pf2 = pf_ref.reshape(HEAD * stride, LANES)
    pb2 = pb_ref.reshape(HEAD * stride, LANES)
    yf2 = yf_ref.reshape(HEAD * t_steps, LANES)
    yb2 = yb_ref.reshape(HEAD * t_steps, LANES)

    @pl.when(i == 0)
    def _():
        S[...] = jnp.concatenate([s0_ref[0], s0_ref[1]], axis=0).T

    V0, BT0, KDT0, KKT0, RT0 = (q * HEAD for q in range(5))
    BR0, KR0 = 5 * HEAD, 5 * HEAD + SUBLANES

    gam = jnp.ones((HEAD, LANES), F32)
    for t in range(t_steps):
        tiles = []
        for q in range(WKV_TILES):
            f = pf2[pl.ds(q * t_steps + t, HEAD, stride=stride), :]
            b = pb2[pl.ds(q * t_steps + (t_steps - 1 - t), HEAD, stride=stride), :]
            tiles.append(jnp.concatenate([f, b], axis=0).T)
        r, vv = tiles[0][:HEAD], tiles[0][HEAD:]
        kk, w = tiles[1][:HEAD], tiles[1][HEAD:]
        kd, b = tiles[2][:HEAD], tiles[2][HEAD:]
        OPS[t, KKT0:KKT0 + HEAD, :] = kk * gam
        gam = gam * w
        inv = 1.0 / gam
        OPS[t, V0:V0 + HEAD, :] = vv
        OPS[t, BT0:BT0 + HEAD, :] = b * inv
        OPS[t, KDT0:KDT0 + HEAD, :] = kd * inv
        OPS[t, RT0:RT0 + HEAD, :] = r * gam
        OPS[t, BR0:BR0 + SUBLANES, :] = _allsum8(b * r)
        OPS[t, KR0:KR0 + SUBLANES, :] = _allsum8(kd * r)
    GAM[...] = gam

    def accumulate(j, state_tile, k, sa, y0):
        return (sa + state_tile * OPS[j, KKT0 + k:KKT0 + k + 1, :],
                y0 + state_tile * OPS[j, RT0 + k:RT0 + k + 1, :])

    sa = jnp.zeros((HEAD, LANES), F32)
    y0 = jnp.zeros((HEAD, LANES), F32)
    for k in range(HEAD):
        sa, y0 = accumulate(0, S[k * HEAD:(k + 1) * HEAD, :], k, sa, y0)
    ACC[0:HEAD, :] = sa
    ACC[HEAD:2 * HEAD, :] = y0

    def step(j, carry):
        jn = jnp.minimum(j + 1, t_steps - 1)
        sa = ACC[0:HEAD, :]
        y0 = ACC[HEAD:2 * HEAD, :]
        vv = OPS[j, V0:V0 + HEAD, :]
        br = jnp.tile(OPS[j, BR0:BR0 + SUBLANES, :], (HEAD // SUBLANES, 1))
        kr = jnp.tile(OPS[j, KR0:KR0 + SUBLANES, :], (HEAD // SUBLANES, 1))
        YS[j] = y0 - sa * br + vv * kr
        san = jnp.zeros((HEAD, LANES), F32)
        y0n = jnp.zeros((HEAD, LANES), F32)
        for k in range(HEAD):
            rows = slice(k * HEAD, (k + 1) * HEAD)
            snew = (S[rows, :] - sa * OPS[j, BT0 + k:BT0 + k + 1, :]
                    + vv * OPS[j, KDT0 + k:KDT0 + k + 1, :])
            S[rows, :] = snew
            san, y0n = accumulate(jn, snew, k, san, y0n)
        ACC[0:HEAD, :] = san
        ACC[HEAD:2 * HEAD, :] = y0n
        return carry

    lax.fori_loop(0, t_steps, step, 0)

    for k in range(HEAD):
        rows = slice(k * HEAD, (k + 1) * HEAD)
        S[rows, :] = S[rows, :] * GAM[k:k + 1, :]

    zeros = jnp.zeros((HEAD, LANES), F32)
    for t in range(t_steps):
        yt = jnp.concatenate([YS[t], zeros], axis=0).T
        yf2[pl.ds(t, HEAD, stride=t_steps), :] = yt[:HEAD]
        yb2[pl.ds(t_steps - 1 - t, HEAD, stride=t_steps), :] = yt[HEAD:]

    @pl.when(i == n - 1)
    def _():
        st = S[...].T
        so_ref[0] = st[0:HEAD]
        so_ref[1] = st[HEAD:2 * HEAD]


def _wkv_scan(pkf, pkb, s0):
    b, nh, _, l, _ = pkf.shape
    gb = HEAD // nh
    assert gb * nh == HEAD and b % gb == 0 and l % SCAN_T == 0
    groups = b // gb
    n = l // SCAN_T
    pk_blk = (gb, nh, WKV_TILES, SCAN_T, LANES)
    y_blk = (gb, nh, SCAN_T, LANES)
    st = pl.BlockSpec((2, HEAD, HEAD * HEAD), lambda g, i: (0, g, 0))
    y_shape = jax.ShapeDtypeStruct((b, nh, l, LANES), F32)
    return pl.pallas_call(
        _wkv_kernel,
        grid=(groups, n),
        in_specs=[pl.BlockSpec(pk_blk, lambda g, i: (g, 0, 0, i, 0)),
                  pl.BlockSpec(pk_blk, lambda g, i: (g, 0, 0, n - 1 - i, 0)), st],
        out_specs=[pl.BlockSpec(y_blk, lambda g, i: (g, 0, i, 0)),
                   pl.BlockSpec(y_blk, lambda g, i: (g, 0, n - 1 - i, 0)), st],
        out_shape=[y_shape, y_shape, jax.ShapeDtypeStruct((2, b * nh, HEAD * HEAD), F32)],
        scratch_shapes=[pltpu.VMEM((HEAD * HEAD, LANES), F32),
                        pltpu.VMEM((SCAN_T, 5 * HEAD + 2 * SUBLANES, LANES), F32),
                        pltpu.VMEM((SCAN_T, HEAD, LANES), F32),
                        pltpu.VMEM((2 * HEAD, LANES), F32), pltpu.VMEM((HEAD, LANES), F32)],
        compiler_params=_cparams("parallel", "arbitrary"),
        name="wkv_scan",
    )(pkf, pkb, s0)


def _rwkv_post_kernel(yf_ref, yb_ref, bonus_ref, g_ref, gng_ref, gnb_ref, o_ref):
    y = yf_ref[0, :, :, 0:HEAD] + yb_ref[0, :, :, 0:HEAD]
    mu = jnp.mean(y, axis=-1, keepdims=True)
    yc = y - mu
    var = jnp.mean(yc * yc, axis=-1, keepdims=True)
    yn = yc * lax.rsqrt(var + GN_EPS)
    yt = jnp.concatenate([yn[h] for h in range(yn.shape[0])], axis=-1)
    o_ref[0] = (yt * gng_ref[...] + gnb_ref[...] + bonus_ref[0]) * g_ref[0]


def _rwkv_post(yf, yb, bonus, g, gn_g, gn_b, tl):
    b, nh, l, _ = yf.shape
    w = nh * HEAD
    hm = pl.BlockSpec((1, nh, tl, LANES), lambda b_, i: (b_, 0, i, 0))
    tok = pl.BlockSpec((1, tl, w), lambda b_, i: (b_, i, 0))
    vec = pl.BlockSpec((1, w), lambda b_, i: (0, 0))
    return pl.pallas_call(
        _rwkv_post_kernel,
        grid=(b, l // tl),
        in_specs=[hm, hm, tok, tok, vec, vec],
        out_specs=tok,
        out_shape=jax.ShapeDtypeStruct((b, l, w), F32),
        compiler_params=_cparams("parallel", "parallel"),
        name="rwkv_post",
    )(yf, yb, bonus, g, gn_g, gn_b)


def _hy_prep_kernel(p_ref, pp_ref, pn_ref, sw_ref, x0_o, u_o, ub_o, *, seq_len, width):
    pc = _conv3(p_ref, pp_ref, pn_ref, sw_ref[...], seq_len)
    w = width
    x0_o[0] = pc[:, 0:w]
    u = pc[:, w:2 * w] * pc[:, 2 * w:3 * w]
    u_o[0] = u
    ub_o[0] = u.astype(BF16)


def _hy_prep(p_h, short_w, tl, seq_len):
    b, l, w3 = p_h.shape
    assert seq_len % tl == 0
    w = w3 // 3
    prev, nxt = _halo_specs(tl, w3, l)
    tok = pl.BlockSpec((1, tl, w), lambda b_, i: (b_, i, 0))
    return pl.pallas_call(
        functools.partial(_hy_prep_kernel, seq_len=seq_len, width=w),
        grid=(b, l // tl),
        in_specs=[pl.BlockSpec((1, tl, w3), lambda b_, i: (b_, i, 0)), prev, nxt,
                  pl.BlockSpec((3, w3), lambda b_, i: (0, 0))],
        out_specs=[tok, tok, tok],
        out_shape=[jax.ShapeDtypeStruct((b, l, w), F32), jax.ShapeDtypeStruct((b, l, w), F32),
                   jax.ShapeDtypeStruct((b, l, w), BF16)],
        compiler_params=_cparams("parallel", "parallel"),
        name="hy_prep",
    )(p_h, p_h, p_h, short_w)


def _hy_mlp_kernel(z_ref, w1_ref, b1_ref, f_ref, w2_ref, b2_ref, w3_ref, dec_ref, h_o):
    z = z_ref[...]
    t01 = z[:, 0:1]
    w1h, w1l = _split2(w1_ref[...])
    w2h, w2l = _split2(w2_ref[...])
    w3h, w3l = _split2(w3_ref[...])
    h = jnp.sin(f_ref[0:1, :] * (_dot3(z, w1h, w1l) + b1_ref[...]))
    h = jnp.sin(f_ref[1:2, :] * (_dot3(h, w2h, w2l) + b2_ref[...]))
    h = _dot3(h, w3h, w3l) * jnp.exp(-t01 * jnp.abs(dec_ref[...]))
    h_o[0] = h.astype(BF16)


def _hy_mlp(z, lp, tl):
    n = z.shape[0]
    c2 = lp["hy_w3"].shape[1]

    def full(a):
        return pl.BlockSpec(a.shape, lambda i: (0,) * a.ndim)

    consts = [lp["hy_w1"], lp["hy_b1"], lp["hy_freq"], lp["hy_w2"], lp["hy_b2"], lp["hy_w3"],
              lp["hy_decay"]]
    out = pl.BlockSpec((1, tl, c2), lambda i: (0, i, 0))
    return pl.pallas_call(
        _hy_mlp_kernel,
        grid=(n // tl,),
        in_specs=[pl.BlockSpec((tl, LANES), lambda i: (i, 0))] + [full(a) for a in consts],
        out_specs=out,
        out_shape=jax.ShapeDtypeStruct((1, n, c2), BF16),
        compiler_params=_cparams("parallel"),
        name="hy_mlp",
    )(z, *consts)


def _dft_filter_kernel(fr_ref, fi_ref, x_ref, hre_o, him_o):
    x = x_ref[0]
    w = x.shape[1] // 2
    xre = _dot(fr_ref[...], x)
    xim = _dot(fi_ref[...], x)
    hre_o[...] = xre[:, :w] + xre[:, w:]
    row = lax.broadcasted_iota(jnp.int32, (xre.shape[0], 1), 0) + pl.program_id(0) * xre.shape[0]
    him_o[...] = jnp.where(row == 0, xim[:, :w] + xim[:, w:], xim[:, :w] - xim[:, w:])


def _dft_filter(mats, h, tf):
    n = h.shape[1]
    w = h.shape[2] // 2
    ft = pl.BlockSpec((tf, n), lambda i: (i, 0))
    xs = pl.BlockSpec((1, n, 2 * w), lambda i: (0, 0, 0))
    out = pl.BlockSpec((tf, w), lambda i: (i, 0))
    return pl.pallas_call(
        _dft_filter_kernel,
        grid=(n // tf,),
        in_specs=[ft, ft, xs],
        out_specs=[out, out],
        out_shape=[jax.ShapeDtypeStruct((n, w), F32)] * 2,
        compiler_params=_cparams("parallel"),
        name="dft_filter",
    )(mats["fre"], mats["fim"], h)


def _dft_signal_kernel(fr_ref, fi_ref, x_ref, hre_ref, him_ref, pr_o, pi_o):
    x = x_ref[0]
    xre = _dot(fr_ref[...], x)
    xim = _dot(fi_ref[...], x)
    hre, him = hre_ref[...], him_ref[...]
    row = lax.broadcasted_iota(jnp.int32, (xre.shape[0], 1), 0) + pl.program_id(1) * xre.shape[0]
    pr_o[0] = jnp.where(row == 0, xre * hre, xre * hre - xim * him).astype(BF16)
    pi_o[0] = jnp.where(row == 0, xim * him, xre * him + xim * hre).astype(BF16)


def _dft_signal(mats, ub, hre, him, tf):
    b, n, w = ub.shape
    ft = pl.BlockSpec((tf, n), lambda b_, i: (i, 0))
    xs = pl.BlockSpec((1, n, w), lambda b_, i: (b_, 0, 0))
    hs = pl.BlockSpec((tf, w), lambda b_, i: (i, 0))
    out = pl.BlockSpec((1, tf, w), lambda b_, i: (b_, i, 0))
    return pl.pallas_call(
        _dft_signal_kernel,
        grid=(b, n // tf),
        in_specs=[ft, ft, xs, hs, hs],
        out_specs=[out] * 2,
        out_shape=[jax.ShapeDtypeStruct((b, n, w), BF16)] * 2,
        compiler_params=_cparams("parallel", "parallel"),
        name="dft_signal",
    )(mats["fre"], mats["fim"], ub, hre, him)


def _dft_inverse_kernel(gr_ref, gi_ref, pr_ref, pi_ref, x0_ref, u_ref, bias_ref, o_ref):
    y = _dot(gr_ref[...], pr_ref[0]) + _dot(gi_ref[...], pi_ref[0])
    o_ref[0] = x0_ref[0] * (y + u_ref[0] * bias_ref[...])


def _dft_inverse(mats, pr, pi, x0, u, bias, tt):
    b, n, w = u.shape
    gt = pl.BlockSpec((tt, n), lambda b_, i: (i, 0))
    ps = pl.BlockSpec((1, n, w), lambda b_, i: (b_, 0, 0))
    tok = pl.BlockSpec((1, tt, w), lambda b_, i: (b_, i, 0))
    return pl.pallas_call(
        _dft_inverse_kernel,
        grid=(b, n // tt),
        in_specs=[gt, gt, ps, ps, tok, tok, pl.BlockSpec((1, w), lambda b_, i: (0, 0))],
        out_specs=tok,
        out_shape=jax.ShapeDtypeStruct((b, n, w), F32),
        compiler_params=_cparams("parallel", "parallel"),
        name="dft_inverse",
    )(mats["gre"], mats["gim"], pr, pi, x0, u, bias)


def _swap_halves(x):
    lane = lax.broadcasted_iota(jnp.int32, x.shape, 1)
    half = HEAD // 2
    return jnp.where(lane % HEAD < half, pltpu.roll(x, LANES - half, 1), pltpu.roll(x, half, 1))


def _attn_prep_kernel(*refs, rope, wq, wkv):
    if rope:
        p_ref, qn_ref, kn_ref, ones_ref, cos_ref, sin_ref, q_o, k_o, v_o, kt_o = refs
    else:
        p_ref, qn_ref, kn_ref, ones_ref, q_o, k_o, v_o, kt_o = refs
    ones_bd = ones_ref[...]

    def norm_rope(x, gain):
        ms = _segsum(x * x, ones_bd) * (1.0 / HEAD)
        xn = x * lax.rsqrt(ms + QK_EPS) * gain
        if rope:
            xn = xn * cos_ref[...] + _swap_halves(xn) * sin_ref[...]
        return xn

    scale = HEAD ** -0.5
    for c in range(wq // LANES):
        xq = norm_rope(p_ref[0, :, c * LANES:(c + 1) * LANES], qn_ref[...]) * scale
        q_o[0, 2 * c] = xq[:, :HEAD].astype(BF16)
        q_o[0, 2 * c + 1] = xq[:, HEAD:].astype(BF16)
    for c in range(wkv // LANES):
        x = p_ref[0, :, wq + c * LANES:wq + (c + 1) * LANES]
        ms = _segsum(x * x, ones_bd) * (1.0 / HEAD)
        xk = x * lax.rsqrt(ms + QK_EPS) * kn_ref[...]
        kt_o[0, :, c * LANES:(c + 1) * LANES] = xk
        if rope:
            xk = xk * cos_ref[...] + _swap_halves(xk) * sin_ref[...]
        k_o[0, 2 * c] = xk[:, :HEAD].astype(BF16)
        k_o[0, 2 * c + 1] = xk[:, HEAD:].astype(BF16)
        xv = p_ref[0, :, wq + wkv + c * LANES:wq + wkv + (c + 1) * LANES]
        v_o[0, 2 * c] = xv[:, :HEAD].astype(BF16)
        v_o[0, 2 * c + 1] = xv[:, HEAD:].astype(BF16)


def _attn_prep(p_a, lp, rope_tabs, tl, wq, wkv):
    b, l, wa = p_a.shape
    nq, nkv = wq // HEAD, wkv // HEAD
    rope = rope_tabs is not None
    vec = pl.BlockSpec((1, LANES), lambda b_, i: (0, 0))
    in_specs = [pl.BlockSpec((1, tl, wa), lambda b_, i: (b_, i, 0)), vec, vec,
                pl.BlockSpec((LANES, LANES), lambda b_, i: (0, 0))]
    args = [p_a, lp["attn_qn"], lp["attn_kn"], lp["ones_pair"]]
    if rope:
        tab = pl.BlockSpec((tl, LANES), lambda b_, i: (i, 0))
        in_specs += [tab, tab]
        args += list(rope_tabs)
    return pl.pallas_call(
        functools.partial(_attn_prep_kernel, rope=rope, wq=wq, wkv=wkv),
        grid=(b, l // tl),
        in_specs=in_specs,
        out_specs=[pl.BlockSpec((1, nq, tl, HEAD), lambda b_, i: (b_, 0, i, 0)),
                   pl.BlockSpec((1, nkv, tl, HEAD), lambda b_, i: (b_, 0, i, 0)),
                   pl.BlockSpec((1, nkv, tl, HEAD), lambda b_, i: (b_, 0, i, 0)),
                   pl.BlockSpec((1, tl, wkv), lambda b_, i: (b_, i, 0))],
        out_shape=[jax.ShapeDtypeStruct((b, nq, l, HEAD), BF16),
                   jax.ShapeDtypeStruct((b, nkv, l, HEAD), BF16),
                   jax.ShapeDtypeStruct((b, nkv, l, HEAD), BF16),
                   jax.ShapeDtypeStruct((b, l, wkv), F32)],
        compiler_params=_cparams("parallel", "parallel"),
        name="attn_prep",
    )(*args)


def _attn_kernel(q_ref, k_ref, v_ref, o_ref):
    k, v = k_ref[0, 0], v_ref[0, 0]
    outs = []
    nh = q_ref.shape[1]

    def qk(h):
        return lax.dot_general(q_ref[0, h], k, (((1,), (1,)), ((), ())), preferred_element_type=F32)

    scores = [qk(h) for h in range(min(2, nh))]
    for h in range(nh):
        s = scores[h]
        m = jnp.max(s, axis=-1, keepdims=True)
        p = jnp.exp(s - m)
        den = jnp.sum(p, axis=-1, keepdims=True)
        if h + 2 < nh:
            scores.append(qk(h + 2))
        outs.append(_dot(p.astype(BF16), v) / den)
    o_ref[0] = jnp.concatenate(outs, axis=-1)


def _attention(q, k, v, tq):
    b, nq, l, _ = q.shape
    nkv, lk = k.shape[1], k.shape[2]
    g = nq // nkv
    kv = pl.BlockSpec((1, 1, lk, HEAD), lambda b_, h, i: (b_, h, 0, 0))
    return pl.pallas_call(
        _attn_kernel,
        grid=(b, nkv, l // tq),
        in_specs=[pl.BlockSpec((1, g, tq, HEAD), lambda b_, h, i: (b_, h, i, 0)), kv, kv],
        out_specs=pl.BlockSpec((1, tq, g * HEAD), lambda b_, h, i: (b_, i, h)),
        out_shape=jax.ShapeDtypeStruct((b, l, nq * HEAD), F32),
        compiler_params=_cparams("parallel", "parallel", "parallel"),
        name="attention",
    )(q, k, v)


def _out_proj_kernel(yr_ref, yh_ref, ya_ref, x_ref, g_ref, w_ref, lng_ref, lnb_ref, o_ref, *, alpha):
    wr, wh = yr_ref.shape[2], yh_ref.shape[2]
    mix = _dot(yr_ref[0].astype(BF16), w_ref[0:wr, :])
    mix = mix + _dot(yh_ref[0].astype(BF16), w_ref[wr:wr + wh, :])
    mix = mix + _dot(ya_ref[0].astype(BF16), w_ref[wr + wh:, :])
    o_ref[0] = _layer_norm(alpha * x_ref[0] + g_ref[0] * mix, lng_ref[...], lnb_ref[...])


def _out_proj(y_r, y_h, y_a, x, gate, w_out, ln_g, ln_b, tl, alpha):
    bx, lx, d = x.shape
    sel = _bm(gate)

    def tok(a):
        return pl.BlockSpec((1, tl, a.shape[2]), lambda b, i: (b, i, 0))

    vec = pl.BlockSpec((1, d), lambda b, i: (0, 0))
    return pl.pallas_call(
        functools.partial(_out_proj_kernel, alpha=alpha),
        grid=(bx, lx // tl),
        in_specs=[tok(y_r), tok(y_h), tok(y_a), tok(x),
                  pl.BlockSpec((1, 1, d), lambda b, i: (sel(b), 0, 0)),
                  pl.BlockSpec(w_out.shape, lambda b, i: (0, 0)), vec, vec],
        out_specs=tok(x),
        out_shape=jax.ShapeDtypeStruct((bx, lx, d), F32),
        compiler_params=_cparams("parallel", "parallel"),
        name="out_proj",
    )(y_r, y_h, y_a, x, gate, w_out, ln_g, ln_b)


def _ffn_kernel(x_ref, xp_ref, xn_ref, sc_ref, sh_ref, g_ref, wa_ref, wb_ref, ca_ref, cb_ref,
                wd_ref, lng_ref, lnb_ref, o_ref, h_s, acc_s, ua_s, ub_s, *, seq_len, alpha):
    tl = x_ref.shape[1]
    i = pl.program_id(1)
    j = pl.program_id(2)
    halo = FFN_HALO

    @pl.when(j == 0)
    def _():
        sc, sh = 1.0 + sc_ref[0], sh_ref[0]
        h_s[0:halo] = (xp_ref[0] * sc + sh).astype(BF16)
        h_s[halo:halo + tl] = (x_ref[0] * sc + sh).astype(BF16)
        h_s[halo + tl:] = (xn_ref[0] * sc + sh).astype(BF16)
        acc_s[...] = jnp.zeros_like(acc_s)

    row = lax.broadcasted_iota(jnp.int32, (tl, 1), 0)
    pos = (i * tl + row) % seq_len
    first = pos == 0
    last = pos == seq_len - 1
    h = h_s[...]

    ua_s[...] = _dot(h, wa_ref[...])
    ub_s[...] = _dot(h, wb_ref[...])

    def conv(u_s, c_ref):
        um = jnp.where(first, 0.0, u_s[halo - 1:halo - 1 + tl, :])
        up = jnp.where(last, 0.0, u_s[halo + 1:halo + 1 + tl, :])
        return um * c_ref[0:1, :] + u_s[halo:halo + tl, :] * c_ref[1:2, :] + up * c_ref[2:3, :]

    a = conv(ua_s, ca_ref)
    b = conv(ub_s, cb_ref)
    f = (a * _sigmoid(a) * b).astype(BF16)
    acc_s[...] += _dot(f, wd_ref[...])

    @pl.when(j == pl.num_programs(2) - 1)
    def _():
        o_ref[0] = _layer_norm(alpha * x_ref[0] + g_ref[0] * acc_s[...], lng_ref[...], lnb_ref[...])


def _ffn(x, sc, sh, gate, w_up, conv_w, w_down, ln_g, ln_b, tl, tn, seq_len, alpha):
    bx, lx, d = x.shape
    dff = w_down.shape[0]
    nj = dff // tn
    sel = _bm(sc)
    prev, nxt = _halo_specs(tl, d, lx, FFN_HALO)
    prev3 = pl.BlockSpec(prev.block_shape, lambda b, i, j: prev.index_map(b, i))
    nxt3 = pl.BlockSpec(nxt.block_shape, lambda b, i, j: nxt.index_map(b, i))
    mod = pl.BlockSpec((1, 1, d), lambda b, i, j: (sel(b), 0, 0))
    vec = pl.BlockSpec((1, d), lambda b, i, j: (0, 0))
    tok = pl.BlockSpec((1, tl, d), lambda b, i, j: (b, i, 0))
    return pl.pallas_call(
        functools.partial(_ffn_kernel, seq_len=seq_len, alpha=alpha),
        grid=(bx, lx // tl, nj),
        in_specs=[tok, prev3, nxt3, mod, mod, mod,
                  pl.BlockSpec((d, tn), lambda b, i, j: (0, j)),
                  pl.BlockSpec((d, tn), lambda b, i, j: (0, j + nj)),
                  pl.BlockSpec((3, tn), lambda b, i, j: (0, j)),
                  pl.BlockSpec((3, tn), lambda b, i, j: (0, j + nj)),
                  pl.BlockSpec((tn, d), lambda b, i, j: (j, 0)), vec, vec],
        out_specs=tok,
        out_shape=jax.ShapeDtypeStruct((bx, lx, d), F32),
        scratch_shapes=[pltpu.VMEM((tl + 2 * FFN_HALO, d), BF16), pltpu.VMEM((tl, d), F32),
                        pltpu.VMEM((tl + 2 * FFN_HALO, tn), F32),
                        pltpu.VMEM((tl + 2 * FFN_HALO, tn), F32)],
        compiler_params=_cparams("parallel", "parallel", "arbitrary"),
        name="conv_ffn",
    )(x, x, x, sc, sh, gate, w_up, w_up, conv_w, conv_w, w_down, ln_g, ln_b)


def _split2_host(x):
    hi = lax.bitcast_convert_type(lax.bitcast_convert_type(x, jnp.uint32) & jnp.uint32(0xFFFF0000), F32)
    return hi.astype(BF16), (x - hi).astype(BF16)


def _dft_mats(n):
    big = 2 * n
    lo = min(n, 64)
    k = jnp.arange(n, dtype=jnp.int32)[:, None]

    def table(t):
        ang = ((k * t[None, :]) % big).astype(F32) * (2.0 * math.pi / big)
        return jnp.cos(ang), jnp.sin(ang)

    ca, sa = table(lo * jnp.arange(n // lo, dtype=jnp.int32))
    cb, sb = table(jnp.arange(lo, dtype=jnp.int32))
    cos = (ca[:, :, None] * cb[:, None, :] - sa[:, :, None] * sb[:, None, :]).reshape(n, n)
    sin = (sa[:, :, None] * cb[:, None, :] + ca[:, :, None] * sb[:, None, :]).reshape(n, n)
    t = jnp.arange(n, dtype=jnp.int32)[None, :]
    fre = cos
    fim = jnp.where(k == 0, jnp.where(t % 2 == 0, 1.0, -1.0), -sin)
    scale = jnp.where(k == 0, 1.0 / big, 2.0 / big)
    return {"fre": fre.astype(BF16), "fim": fim.astype(BF16),
            "gre": (fre * scale).T.astype(BF16), "gim": (fim * scale).T.astype(BF16)}


def _hyena_features(n, n_bands):
    t01 = jnp.linspace(0.0, 1.0, n, dtype=F32)[:, None]
    pos = jnp.arange(n, dtype=F32)[:, None]
    bands = jnp.linspace(1e-4, n_bands - 1, n_bands, dtype=F32)[None, :]
    ang = (2.0 * math.pi / n) * pos * bands
    z = jnp.concatenate([t01, jnp.cos(ang), -jnp.sin(ang)], -1)
    return jnp.pad(z, ((0, 0), (0, LANES - z.shape[1])))


def _rope_tables(n_tokens):
    rows = n_tokens // GRID_W
    row = jnp.repeat(jnp.arange(rows, dtype=F32), GRID_W)
    col = jnp.tile(jnp.arange(GRID_W, dtype=F32), rows)
    n_freq = HEAD // 4
    inv = ROPE_THETA ** (-jnp.arange(n_freq, dtype=F32) / n_freq)
    ang = jnp.concatenate([row[:, None] * inv, col[:, None] * inv], -1)
    cos, sin = jnp.cos(ang), jnp.sin(ang)
    cos2 = jnp.tile(jnp.concatenate([cos, cos], -1), (1, LANES // HEAD))
    sin2 = jnp.tile(jnp.concatenate([-sin, sin], -1), (1, LANES // HEAD))
    return cos2, sin2


def _block_ones(n, group):
    idx = jnp.arange(n) // group
    return (idx[:, None] == idx[None, :]).astype(BF16)


def _pad_to(a, shape):
    return jnp.pad(a, [(0, s - d) for d, s in zip(a.shape, shape)])


def _block_diag2(a, b):
    za = jnp.zeros((a.shape[0], b.shape[1]), a.dtype)
    zb = jnp.zeros((b.shape[0], a.shape[1]), a.dtype)
    return jnp.concatenate([jnp.concatenate([a, za], 1), jnp.concatenate([zb, b], 1)], 0)


def _layer_params(l, P, dims):
    w_r, w_h, wq, wkv = dims["w_rwkv"], dims["w_hyena"], dims["wq"], dims["wkv"]
    rwkv_cols = 3 * w_r + 2 * P["rwkv_w2"].shape[2] + 2 * P["rwkv_a2"].shape[2] + P["rwkv_g2"].shape[1]
    rwkv_pad = -(-rwkv_cols // (2 * LANES)) * (2 * LANES)
    hy_cols = 3 * w_h
    w_in = P["w_in"][l]
    lp = {"rwkv_w": w_r}
    lp["w_in_r"] = _pad_to(w_in[:, :rwkv_cols], (w_in.shape[0], rwkv_pad)).astype(BF16)
    lp["w_in_h"] = w_in[:, rwkv_cols:rwkv_cols + hy_cols].astype(BF16)
    lp["w_in_a"] = w_in[:, rwkv_cols + hy_cols:].astype(BF16)
    lp["rwkv_shift"] = _pad_to(P["rwkv_shift"][l], (3, rwkv_pad))
    lp["rwkv_w0"] = P["rwkv_w0"][l]
    lp["rwkv_a0"] = P["rwkv_a0"][l]
    for nm in ("rwkv_kk", "rwkv_ka", "rwkv_rk", "rwkv_gn_g", "rwkv_gn_b", "hy_bias",
               "ln1_g", "ln1_b", "ln2_g", "ln2_b"):
        lp[nm] = P[nm][l][None, :]
    lp["w2_hi"], lp["w2_lo"] = _split2_host(_block_diag2(P["rwkv_w2"][l, 0], P["rwkv_w2"][l, 1]))
    lp["a2_hi"], lp["a2_lo"] = _split2_host(_block_diag2(P["rwkv_a2"][l, 0], P["rwkv_a2"][l, 1]))
    lp["g2_hi"], lp["g2_lo"] = _split2_host(P["rwkv_g2"][l])
    lp["ones_head"] = _block_ones(w_r, HEAD)
    lp["ones_pair"] = _block_ones(LANES, HEAD)
    lp["hy_short"] = P["hy_short"][l]
    ffn_w = P["hy_w1"].shape[2]
    lp["hy_w1"] = _pad_to(P["hy_w1"][l], (LANES, LANES))
    lp["hy_b1"] = _pad_to(P["hy_b1"][l][None, :], (1, LANES))
    lp["hy_freq"] = _pad_to(P["hy_freq"][l], (2, LANES))
    lp["hy_w2"] = _pad_to(P["hy_w2"][l], (LANES, LANES))
    lp["hy_b2"] = _pad_to(P["hy_b2"][l][None, :], (1, LANES))
    lp["hy_w3"] = _pad_to(P["hy_w3"][l], (LANES, 2 * w_h))
    lp["hy_decay"] = P["hy_decay"][l].reshape(1, 2 * w_h)
    del ffn_w
    lp["attn_qn"] = jnp.tile(P["attn_qn"][l], LANES // HEAD)[None, :]
    lp["attn_kn"] = jnp.tile(P["attn_kn"][l], LANES // HEAD)[None, :]
    lp["w_out"] = P["w_out"][l].astype(BF16)
    lp["ffn_up"] = P["ffn_up"][l].astype(BF16)
    lp["ffn_conv"] = P["ffn_conv"][l]
    lp["ffn_down"] = P["ffn_down"][l].astype(BF16)
    return lp


def _trunk_layer(x, mod6, lp, dims, rope_tabs, ctx_kv, s0, dft, z_feat, tiles):
    b, l, d = x.shape
    sh1, sc1, g1, sh2, sc2, g2 = mod6
    shared = sh1.shape[0] == 1
    alpha = dims["alpha"]
    w_r, w_h, wq, wkv = dims["w_rwkv"], dims["w_hyena"], dims["wq"], dims["wkv"]
    nh = w_r // HEAD
    tl_mm, tl_ew = tiles["mm"], tiles["ew"]

    xm = x.reshape(1, b * l, d) if shared else x
    p_r = _mod_proj(xm, sc1, sh1, lp["w_in_r"], tl_mm).reshape(b, l, -1)
    p_h = _mod_proj(xm, sc1, sh1, lp["w_in_h"], tl_mm).reshape(b, l, -1)
    p_a = _mod_proj(xm, sc1, sh1, lp["w_in_a"], tl_mm).reshape(b, l, -1)

    pkf, pkb, bonus, gate = _rwkv_prep(p_r, lp, tl_ew, l)
    yf, yb, s_fin = _wkv_scan(pkf, pkb, s0)
    y_r = _rwkv_post(yf, yb, bonus, gate, lp["rwkv_gn_g"], lp["rwkv_gn_b"], tl_ew)

    x0, u, ub = _hy_prep(p_h, lp["hy_short"], tl_ew, l)
    tf = min(l, 256)
    hre, him = _dft_filter(dft, _hy_mlp(z_feat, lp, tf), tf)
    pr, pi = _dft_signal(dft, ub, hre, him, tf)
    y_h = _dft_inverse(dft, pr, pi, x0, u, lp["hy_bias"], tf)

    q, k, vv, k_tok = _attn_prep(p_a, lp, rope_tabs, tl_ew, wq, wkv)
    if ctx_kv is not None:
        ck, cv = ctx_kv
        k = jnp.concatenate([k, jnp.swapaxes(ck, 1, 2).astype(BF16)], axis=2)
        vv = jnp.concatenate([vv, jnp.swapaxes(cv, 1, 2).astype(BF16)], axis=2)
    y_a = _attention(q, k, vv, tiles["tq"])

    def m(a):
        return a.reshape(1, b * l, a.shape[-1]) if shared else a

    x1 = _out_proj(m(y_r), m(y_h), m(y_a), xm, g1, lp["w_out"], lp["ln1_g"], lp["ln1_b"],
                   tl_mm, alpha)
    x2 = _ffn(x1, sc2, sh2, g2, lp["ffn_up"], lp["ffn_conv"], lp["ffn_down"],
              lp["ln2_g"], lp["ln2_b"], tl_mm, tiles["ffn_tn"], l, alpha)
    v_tok = p_a[..., wq + wkv:]
    return x2.reshape(b, l, d), k_tok, v_tok, s_fin


def kernel(x_prompt, x_sample, cache_k, cache_v, state_rwkv, c, c_ctx, w_mod, b_mod, w_in, rwkv_shift, rwkv_w0, rwkv_w2, rwkv_a0, rwkv_a2, rwkv_kk, rwkv_ka, rwkv_rk, rwkv_g2, rwkv_gn_g, rwkv_gn_b, hy_short, hy_w1, hy_b1, hy_freq, hy_w2, hy_b2, hy_w3, hy_decay, hy_bias, attn_qn, attn_kn, w_out, ln1_g, ln1_b, ln2_g, ln2_b, ffn_up, ffn_conv, ffn_down):
    P = dict(w_in=w_in, rwkv_shift=rwkv_shift, rwkv_w0=rwkv_w0, rwkv_w2=rwkv_w2, rwkv_a0=rwkv_a0,
             rwkv_a2=rwkv_a2, rwkv_kk=rwkv_kk, rwkv_ka=rwkv_ka, rwkv_rk=rwkv_rk, rwkv_g2=rwkv_g2,
             rwkv_gn_g=rwkv_gn_g, rwkv_gn_b=rwkv_gn_b, hy_short=hy_short, hy_w1=hy_w1, hy_b1=hy_b1,
             hy_freq=hy_freq, hy_w2=hy_w2, hy_b2=hy_b2, hy_w3=hy_w3, hy_decay=hy_decay,
             hy_bias=hy_bias, attn_qn=attn_qn, attn_kn=attn_kn, w_out=w_out, ln1_g=ln1_g,
             ln1_b=ln1_b, ln2_g=ln2_g, ln2_b=ln2_b, ffn_up=ffn_up, ffn_conv=ffn_conv,
             ffn_down=ffn_down)
    depth, d = w_mod.shape[0], w_mod.shape[1]
    bc, lc, _ = x_prompt.shape
    bd, ld, _ = x_sample.shape
    nkv = cache_k.shape[3]
    nh = state_rwkv.shape[3]
    w_r = nh * HEAD
    w_h = hy_bias.shape[1]
    wkv = nkv * HEAD
    wq = w_in.shape[2] - (3 * w_r + 2 * rwkv_w2.shape[2] + 2 * rwkv_a2.shape[2] + rwkv_g2.shape[1]) \
        - 3 * w_h - 2 * wkv
    dims = dict(w_rwkv=w_r, w_hyena=w_h, wq=wq, wkv=wkv, alpha=(2 * depth) ** 0.25)

    rows = -(-(bd + 1) // SUBLANES) * SUBLANES
    cond = _pad_to(jnp.concatenate([c, c_ctx[None, :]], 0), (rows, d))
    mod = _modulation(cond, w_mod, b_mod)

    rope_tabs = _rope_tables(ld)
    dft_c, dft_d = _dft_mats(lc), _dft_mats(ld)
    n_bands = (hy_w1.shape[1] - 1) // 2
    z_c, z_d = _hyena_features(lc, n_bands), _hyena_features(ld, n_bands)
    tiles_c = dict(mm=min(512, bc * lc), ew=min(256, lc), tq=min(256, lc), ffn_tn=512)
    tiles_d = dict(mm=min(512, ld), ew=min(256, ld), tq=min(512, ld), ffn_tn=512)

    xp, xs = x_prompt, x_sample
    zero_state = jnp.zeros((2, bc * nh, HEAD * HEAD), F32)
    new_k, new_v, new_s = [], [], []
    for l in range(depth):
        lp = _layer_params(l, P, dims)
        mod_d = [m[:bd, None, :] for m in jnp.split(mod[l], 6, axis=-1)]
        mod_c = [m[bd:bd + 1, None, :] for m in jnp.split(mod[l], 6, axis=-1)]
        xp, k_c, v_c, s_c = _trunk_layer(xp, mod_c, lp, dims, None, None, zero_state, dft_c, z_c, tiles_c)
        new_k.append(k_c.reshape(bc, lc, nkv, HEAD))
        new_v.append(v_c.reshape(bc, lc, nkv, HEAD))
        new_s.append(jnp.transpose(s_c.reshape(2, bc, nh, HEAD, HEAD), (1, 0, 2, 4, 3)))
        s0 = jnp.transpose(state_rwkv[:, l], (1, 0, 2, 4, 3)).reshape(2, bd * nh, HEAD * HEAD)
        xs, _, _, _ = _trunk_layer(xs, mod_d, lp, dims, rope_tabs, (cache_k[:, l], cache_v[:, l]),
                                   s0, dft_d, z_d, tiles_d)
    return (xp, xs, jnp.stack(new_k, axis=1), jnp.stack(new_v, axis=1), jnp.stack(new_s, axis=1))
```
